```python
import jax, jax.numpy as jnp
from jax import lax
import numpy as np

D_MODEL = 1024
BATCH = 4
SEQ = 8192
DEPTH = 1
DEC_BATCH = 32
DEC_SEQ = 1
PAST_LEN = 16384
PAGE_SIZE = 128

N_HEADS = 8
HEAD_DIM = 64
ATTN_WIDTH = N_HEADS * HEAD_DIM
IDX_HEADS = 4
IDX_DIM = 64
TOPK_MAX = 256
QUERY_BLOCK = 128
ROPE_THETA = 500000.0
ROPE_FRACTION = 4
CHUNK = 128
GMLP_GROUPS = 4
GMLP_WIDTH = D_MODEL // 2
N_GROUPS = 4
EXPERTS_PER_GROUP = 8
EXPERT_DIM = D_MODEL // 4
TOP_K_IN_GROUP = 2
LN_EPS = 1e-5
DEEPNORM_ALPHA = (2 * DEPTH) ** 0.25
DEEPNORM_BETA = (8 * DEPTH) ** -0.25
NEG_INF = -1e30

kernel_name = 'hybrid_dsa_gmlp_hmoe_decode_step'


def _proj_sizes():
    return [ATTN_WIDTH, ATTN_WIDTH, ATTN_WIDTH, IDX_HEADS * IDX_DIM, IDX_DIM, IDX_HEADS,
            GMLP_WIDTH, GMLP_WIDTH, D_MODEL, D_MODEL]


def layer_norm(x, g, b):
    xf = x.astype(jnp.float32)
    mu = jnp.mean(xf, axis=-1, keepdims=True)
    var = jnp.mean(jnp.square(xf - mu), axis=-1, keepdims=True)
    y = (xf - mu) * lax.rsqrt(var + LN_EPS) * g.astype(jnp.float32) + b.astype(jnp.float32)
    return y.astype(x.dtype)


def rope_partial(x, pos):
    rot = x.shape[-1] // ROPE_FRACTION
    half = rot // 2
    inv_freq = jnp.power(jnp.float32(ROPE_THETA), -jnp.arange(0, rot, 2, dtype=jnp.float32) / rot)
    ang = pos.astype(jnp.float32)[:, None] * inv_freq[None, :]
    cos = jnp.cos(ang)[None, :, None, :]
    sin = jnp.sin(ang)[None, :, None, :]
    xr = x[..., :rot].astype(jnp.float32)
    x1, x2 = xr[..., :half], xr[..., half:]
    rotated = jnp.concatenate([x1 * cos - x2 * sin, x2 * cos + x1 * sin], axis=-1).astype(x.dtype)
    return jnp.concatenate([rotated, x[..., rot:]], axis=-1)


def project_inputs(x, pos, w_in, gate_bias):
    B, T, _ = x.shape
    offsets = [int(o) for o in np.cumsum(_proj_sizes())[:-1]]
    z = x @ w_in
    q, k, v, qi, ki, wi, u, vb, ga, gb = jnp.split(z, offsets, axis=-1)
    q = rope_partial(q.reshape(B, T, N_HEADS, HEAD_DIM), pos)
    k = rope_partial(k.reshape(B, T, N_HEADS, HEAD_DIM), pos)
    v = v.reshape(B, T, N_HEADS, HEAD_DIM)
    qi = rope_partial(qi.reshape(B, T, IDX_HEADS, IDX_DIM), pos)
    ki = rope_partial(ki[:, :, None, :], pos)[:, :, 0, :]
    ga = jax.nn.sigmoid(ga + gate_bias[0])
    gb = jax.nn.sigmoid(gb + gate_bias[1])
    return q, k, v, qi, ki, wi, u, vb, ga, gb


def index_scores(qi, wi, ki):
    dots = jnp.einsum('bthd,bsd->bths', qi, ki).astype(jnp.float32)
    scale = (IDX_DIM ** -0.5) * (IDX_HEADS ** -0.5)
    return jnp.einsum('bths,bth->bts', jax.nn.relu(dots), wi.astype(jnp.float32)) * scale


def take_rows(rows, idx):
    return jax.vmap(lambda r, i: r[i])(rows, idx)


def sparse_attend(q, ks, vs, valid):
    logits = jnp.einsum('bthd,btkhd->bthk', q, ks).astype(jnp.float32) * (HEAD_DIM ** -0.5)
    logits = jnp.where(valid[:, :, None, :], logits, NEG_INF)
    p = jax.nn.softmax(logits, axis=-1).astype(vs.dtype)
    out = jnp.einsum('bthk,btkhd->bthd', p, vs)
    return out.reshape(out.shape[0], out.shape[1], ATTN_WIDTH)


def prompt_attention(q, k, v, qi, ki, wi):
    B, S = q.shape[0], q.shape[1]
    topk = min(TOPK_MAX, S // 4)
    nblk = S // QUERY_BLOCK
    key_pos = jnp.arange(S)

    def to_blocks(a):
        return jnp.swapaxes(a.reshape((B, nblk, QUERY_BLOCK) + a.shape[2:]), 0, 1)

    def one_block(args):
        qb, qib, wib, start = args
        qpos = start + jnp.arange(QUERY_BLOCK)
        sc = index_scores(qib, wib, ki)
        sc = jnp.where(key_pos[None, None, :] <= qpos[None, :, None], sc, -jnp.inf)
        _, idx = lax.top_k(sc, topk)
        valid = idx <= qpos[None, :, None]
        return sparse_attend(qb, take_rows(k, idx), take_rows(v, idx), valid)

    starts = jnp.arange(nblk, dtype=jnp.int32) * QUERY_BLOCK
    out = lax.map(one_block, (to_blocks(q), to_blocks(qi), to_blocks(wi), starts))
    return jnp.swapaxes(out, 0, 1).reshape(B, S, ATTN_WIDTH)


def sample_attention(q, k_new, v_new, qi, ki_new, wi, cache_k, cache_v, cache_kidx, page_table):
    DB, T = q.shape[0], q.shape[1]
    past = page_table.shape[1] * PAGE_SIZE
    L = past + T
    topk = min(TOPK_MAX, L // 4)
    ki_past = cache_kidx[page_table].reshape(DB, past, IDX_DIM)
    ki_all = jnp.concatenate([ki_past, ki_new], axis=1)
    qpos = past + jnp.arange(T)
    sc = index_scores(qi, wi, ki_all)
    sc = jnp.where(jnp.arange(L)[None, None, :] <= qpos[None, :, None], sc, -jnp.inf)
    _, idx = lax.top_k(sc, topk)
    valid = idx <= qpos[None, :, None]
    from_past = idx < past
    ip = jnp.minimum(idx, past - 1)
    phys = jax.vmap(lambda pt, i: pt[i // PAGE_SIZE])(page_table, ip)
    off = ip % PAGE_SIZE
    inew = jnp.clip(idx - past, 0, T - 1)
    sel = from_past[..., None, None]
    ks = jnp.where(sel, cache_k[phys, off], take_rows(k_new, inew))
    vs = jnp.where(sel, cache_v[phys, off], take_rows(v_new, inew))
    return sparse_attend(q, ks, vs, valid)


def _spatial_weights(w_spatial):
    return w_spatial * jnp.tril(jnp.ones((CHUNK, CHUNK), w_spatial.dtype))


def gmlp_prompt(u, vb, ln_v_g, ln_v_b, w_spatial, b_spatial):
    B, S, _ = u.shape
    vn = layer_norm(vb, ln_v_g, ln_v_b)
    vr = vn.reshape(B, S // CHUNK, CHUNK, GMLP_GROUPS, GMLP_WIDTH // GMLP_GROUPS)
    mix = jnp.einsum('gts,bcsgd->bctgd', _spatial_weights(w_spatial), vr)
    mix = mix + b_spatial.T[None, None, :, :, None]
    return u * mix.reshape(B, S, GMLP_WIDTH)


def gmlp_sample(u, vb, ln_v_g, ln_v_b, w_spatial, b_spatial):
    B, T, _ = u.shape
    vn = layer_norm(vb, ln_v_g, ln_v_b)
    vr = vn.reshape(B, T, GMLP_GROUPS, GMLP_WIDTH // GMLP_GROUPS)
    ws = _spatial_weights(w_spatial)[:, :T, :T]
    mix = jnp.einsum('gts,bsgd->btgd', ws, vr) + b_spatial[:, :T].T[None, :, :, None]
    return u * mix.reshape(B, T, GMLP_WIDTH), vn


def hierarchical_moe(x, w_router_group, b_router_group, w_router_expert, b_router_expert,
                     w_expert_gate, w_expert_up, w_expert_down):
    xt = x.reshape(-1, D_MODEL)
    lg = (xt @ w_router_group + b_router_group).astype(jnp.float32)
    gsel = jnp.argmax(lg, axis=-1)
    pg = jax.nn.softmax(lg, axis=-1)
    pgsel = jnp.take_along_axis(pg, gsel[:, None], axis=-1)[:, 0]
    le = jnp.einsum('td,gde->tge', xt, w_router_expert) + b_router_expert[None]
    le_sel = jnp.take_along_axis(le, gsel[:, None, None], axis=1)[:, 0].astype(jnp.float32)
    vals, eidx = lax.top_k(le_sel, TOP_K_IN_GROUP)
    pw = jax.nn.softmax(vals, axis=-1) * pgsel[:, None]
    comb = jnp.sum(jax.nn.one_hot(eidx, EXPERTS_PER_GROUP, dtype=jnp.float32) * pw[..., None], axis=1)
    out = jnp.zeros_like(xt)
    for g in range(N_GROUPS):
        wg = (comb * (gsel == g)[:, None].astype(jnp.float32)).astype(xt.dtype)
        hg = jax.nn.silu(jnp.einsum('td,edf->tef', xt, w_expert_gate[g])) * \
            jnp.einsum('td,edf->tef', xt, w_expert_up[g])
        out = out + jnp.einsum('tef,efd->td', hg * wg[:, :, None], w_expert_down[g])
    return out.reshape(x.shape)


def merge_and_channel_mix(x, attn_out, gm_out, ga, gb, w_branch_a, w_branch_b, w_o, ln1_g, ln1_b,
                          w_router_group, b_router_group, w_router_expert, b_router_expert,
                          w_expert_gate, w_expert_up, w_expert_down, ln2_g, ln2_b):
    merged = ga * (attn_out @ w_branch_a) + gb * (gm_out @ w_branch_b)
    h = layer_norm(DEEPNORM_ALPHA * x + merged @ w_o, ln1_g, ln1_b)
    f = hierarchical_moe(h, w_router_group, b_router_group, w_router_expert, b_router_expert,
                         w_expert_gate, w_expert_up, w_expert_down)
    return layer_norm(DEEPNORM_ALPHA * h + f, ln2_g, ln2_b)


def setup_inputs(seed: int = 0) -> dict:
    key = jax.random.key(seed)
    ks = jax.random.split(key, 32)
    f32 = jnp.float32
    n_pages = PAST_LEN // PAGE_SIZE
    n_used = DEC_BATCH * n_pages
    n_pool = n_used + n_used // 4

    def nrm(k, shape, scale):
        return jax.random.normal(k, shape, f32) * scale

    sizes = _proj_sizes()
    col_scale = jnp.concatenate([jnp.full((s,), DEEPNORM_BETA if i == 2 else 1.0, f32)
                                 for i, s in enumerate(sizes)])
    return {
        'x_prompt': nrm(ks[0], (BATCH, SEQ, D_MODEL), 1.0),
        'x_sample': nrm(ks[1], (DEC_BATCH, DEC_SEQ, D_MODEL), 1.0),
        'cache_k': nrm(ks[2], (n_pool, PAGE_SIZE, N_HEADS, HEAD_DIM), 1.0),
        'cache_v': nrm(ks[3], (n_pool, PAGE_SIZE, N_HEADS, HEAD_DIM), 1.0),
        'cache_kidx': nrm(ks[4], (n_pool, PAGE_SIZE, IDX_DIM), 1.0),
        'page_table': jax.random.permutation(ks[5], n_pool)[:n_used].reshape(DEC_BATCH, n_pages).astype(jnp.int32),
        'w_in': nrm(ks[6], (D_MODEL, sum(sizes)), D_MODEL ** -0.5) * col_scale[None, :],
        'gate_bias': nrm(ks[7], (2, D_MODEL), 0.1),
        'ln_v_g': 1.0 + nrm(ks[8], (GMLP_WIDTH,), 0.05),
        'ln_v_b': nrm(ks[9], (GMLP_WIDTH,), 0.05),
        'w_spatial': nrm(ks[10], (GMLP_GROUPS, CHUNK, CHUNK), CHUNK ** -0.5),
        'b_spatial': 1.0 + nrm(ks[11], (GMLP_GROUPS, CHUNK), 0.1),
        'w_branch_a': nrm(ks[12], (ATTN_WIDTH, D_MODEL), ATTN_WIDTH ** -0.5 * DEEPNORM_BETA),
        'w_branch_b': nrm(ks[13], (GMLP_WIDTH, D_MODEL), GMLP_WIDTH ** -0.5 * DEEPNORM_BETA),
        'w_o': nrm(ks[14], (D_MODEL, D_MODEL), D_MODEL ** -0.5 * DEEPNORM_BETA),
        'ln1_g': 1.0 + nrm(ks[15], (D_MODEL,), 0.05),
        'ln1_b': nrm(ks[16], (D_MODEL,), 0.05),
        'w_router_group': nrm(ks[17], (D_MODEL, N_GROUPS), D_MODEL ** -0.5),
        'b_router_group': nrm(ks[18], (N_GROUPS,), 0.01),
        'w_router_expert': nrm(ks[19], (N_GROUPS, D_MODEL, EXPERTS_PER_GROUP), D_MODEL ** -0.5),
        'b_router_expert': nrm(ks[20], (N_GROUPS, EXPERTS_PER_GROUP), 0.01),
        'w_expert_gate': nrm(ks[21], (N_GROUPS, EXPERTS_PER_GROUP, D_MODEL, EXPERT_DIM), D_MODEL ** -0.5),
        'w_expert_up': nrm(ks[22], (N_GROUPS, EXPERTS_PER_GROUP, D_MODEL, EXPERT_DIM), D_MODEL ** -0.5),
        'w_expert_down': nrm(ks[23], (N_GROUPS, EXPERTS_PER_GROUP, EXPERT_DIM, D_MODEL), EXPERT_DIM ** -0.5 * DEEPNORM_BETA),
        'ln2_g': 1.0 + nrm(ks[24], (D_MODEL,), 0.05),
        'ln2_b': nrm(ks[25], (D_MODEL,), 0.05),
    }


def reference(x_prompt, x_sample, cache_k, cache_v, cache_kidx, page_table, w_in, gate_bias,
              ln_v_g, ln_v_b, w_spatial, b_spatial, w_branch_a, w_branch_b, w_o, ln1_g, ln1_b,
              w_router_group, b_router_group, w_router_expert, b_router_expert,
              w_expert_gate, w_expert_up, w_expert_down, ln2_g, ln2_b):
    pos_p = jnp.arange(x_prompt.shape[1], dtype=jnp.int32)
    qp, kp, vp, qip, kip, wip, up, vbp, gap, gbp = project_inputs(x_prompt, pos_p, w_in, gate_bias)
    attn_p = prompt_attention(qp, kp, vp, qip, kip, wip)
    gm_p = gmlp_prompt(up, vbp, ln_v_g, ln_v_b, w_spatial, b_spatial)
    y_prompt = merge_and_channel_mix(x_prompt, attn_p, gm_p, gap, gbp, w_branch_a, w_branch_b, w_o,
                                     ln1_g, ln1_b, w_router_group, b_router_group, w_router_expert,
                                     b_router_expert, w_expert_gate, w_expert_up, w_expert_down,
                                     ln2_g, ln2_b)
    past = page_table.shape[1] * PAGE_SIZE
    pos_s = past + jnp.arange(x_sample.shape[1], dtype=jnp.int32)
    qs, ks, vs, qis, kis, wis, us, vbs, gas, gbs = project_inputs(x_sample, pos_s, w_in, gate_bias)
    attn_s = sample_attention(qs, ks, vs, qis, kis, wis, cache_k, cache_v, cache_kidx, page_table)
    gm_s, chunk_v_s = gmlp_sample(us, vbs, ln_v_g, ln_v_b, w_spatial, b_spatial)
    y_sample = merge_and_channel_mix(x_sample, attn_s, gm_s, gas, gbs, w_branch_a, w_branch_b, w_o,
                                     ln1_g, ln1_b, w_router_group, b_router_group, w_router_expert,
                                     b_router_expert, w_expert_gate, w_expert_up, w_expert_down,
                                     ln2_g, ln2_b)
    return (y_prompt, y_sample, kp, vp, kip, ks, vs, kis, chunk_v_s)
```

```python
import functools

import jax
import jax.numpy as jnp
import numpy as np
from jax import lax
from jax.experimental import pallas as pl
from jax.experimental.pallas import tpu as pltpu

F32 = jnp.float32
BF16 = jnp.bfloat16

D_MODEL = 1024
N_HEADS = 8
HEAD_DIM = 64
ATTN_WIDTH = N_HEADS * HEAD_DIM
IDX_HEADS = 4
IDX_DIM = 64
TOPK = 256
ROPE_THETA = 500000.0
ROPE_ROT = HEAD_DIM // 4
ROPE_HALF = ROPE_ROT // 2
CHUNK = 128
GMLP_GROUPS = 4
GMLP_WIDTH = D_MODEL // 2
N_GROUPS = 4
EXPERTS_PER_GROUP = 8
N_EXPERTS = N_GROUPS * EXPERTS_PER_GROUP
EXPERT_DIM = D_MODEL // 4
PAGE_SIZE = 128
LN_EPS = 1e-5
DEEPNORM_ALPHA = 2.0 ** 0.25
MASK_BIAS = -1e30

VMEM_LIMIT_V7X = 56 * 1024 * 1024
LANES = 128

_ROW_Q = 0
_ROW_K = _ROW_Q + ATTN_WIDTH
_ROW_V = _ROW_K + ATTN_WIDTH
_ROW_QI = _ROW_V + ATTN_WIDTH
_ROW_KI = _ROW_QI + IDX_HEADS * IDX_DIM
_ROW_WI = _ROW_KI + IDX_DIM
_WI_PAD = 16
_ROWS_A = _ROW_WI + _WI_PAD
_V_ROWS = HEAD_DIM + 16
LOG2_E = 1.4426950408889634


def _cparams(sem):
    return pltpu.CompilerParams(dimension_semantics=sem, vmem_limit_bytes=VMEM_LIMIT_V7X)


def _layer_norm(x, g, b):
    mu = jnp.mean(x, axis=-1, keepdims=True)
    xc = x - mu
    var = jnp.mean(xc * xc, axis=-1, keepdims=True)
    return xc * lax.rsqrt(var + LN_EPS) * g + b


def _rope_rows(seg, cos, sin):
    x1 = seg[0:ROPE_HALF]
    x2 = seg[ROPE_HALF:ROPE_ROT]
    return jnp.concatenate([x1 * cos - x2 * sin, x2 * cos + x1 * sin, seg[ROPE_ROT:]], axis=0)


def _split_bf16(a):
    hi = a.astype(BF16)
    return hi, (a - hi.astype(F32)).astype(BF16)


def _proj_attn_kernel(x_ref, wt_ref, *rest, tk, q_scale, precise_idx):
    if precise_idx:
        wtlo_ref, rest = rest[0], rest[1:]
    (cos_ref, sin_ref,
     qz_ref, kn_ref, knb_ref, vn_ref, vtb_ref, qit_ref, kin_ref, kinb_ref, wit_ref) = rest
    x = x_ref[0]
    xb = x.astype(BF16)
    tm = xb.shape[0]
    cos = cos_ref[...]
    sin = sin_ref[...]
    nt = (((1,), (1,)), ((), ()))

    def proj_t(r0, n):
        return lax.dot_general(wt_ref[r0:r0 + n, :], xb, nt, preferred_element_type=F32)

    if precise_idx:
        xlo = (x - xb.astype(F32)).astype(BF16)

        def proj_idx(r0, n):
            w_hi = wt_ref[r0:r0 + n, :]
            return (lax.dot_general(w_hi, xb, nt, preferred_element_type=F32)
                    + lax.dot_general(w_hi, xlo, nt, preferred_element_type=F32)
                    + lax.dot_general(wtlo_ref[r0:r0 + n, :], xb, nt, preferred_element_type=F32))
    else:
        proj_idx = proj_t

    q_t = proj_t(_ROW_Q, ATTN_WIDTH)
    zeros = jnp.zeros((HEAD_DIM, tm), F32)
    for h in range(N_HEADS):
        rot = _rope_rows(q_t[h * HEAD_DIM:(h + 1) * HEAD_DIM], cos, sin) * q_scale
        pair = [rot, zeros] if h % 2 == 0 else [zeros, rot]
        qz_ref[0, h] = jnp.concatenate(pair, axis=0).astype(BF16)

    k_t = proj_t(_ROW_K, ATTN_WIDTH)
    k_rot = jnp.concatenate(
        [_rope_rows(k_t[h * HEAD_DIM:(h + 1) * HEAD_DIM], cos, sin) for h in range(N_HEADS)], axis=0)
    k_n = k_rot.T
    kn_ref[0] = k_n
    knb_ref[0] = k_n.astype(BF16)

    v_t = proj_t(_ROW_V, ATTN_WIDTH)
    vn_ref[0] = v_t.T
    ones = jnp.ones((_V_ROWS - HEAD_DIM, tk), F32)
    for c in range(tm // tk):
        v_c = v_t[:, c * tk:(c + 1) * tk]
        vtb_ref[0, c] = jnp.concatenate(
            [piece for h in range(N_HEADS) for piece in (v_c[h * HEAD_DIM:(h + 1) * HEAD_DIM], ones)],
            axis=0).astype(BF16)

    qi_t = proj_idx(_ROW_QI, IDX_HEADS * IDX_DIM)
    qit_ref[0] = jnp.concatenate(
        [_rope_rows(qi_t[h * IDX_DIM:(h + 1) * IDX_DIM], cos, sin) for h in range(IDX_HEADS)],
        axis=0).astype(qit_ref.dtype)

    ki_n = _rope_rows(proj_idx(_ROW_KI, IDX_DIM), cos, sin).T
    kin_ref[0] = ki_n
    kinb_ref[0] = ki_n.astype(BF16)

    wit_ref[0] = proj_idx(_ROW_WI, _WI_PAD)[0:8]


def _proj_attn(x, wt_a, cos_t, sin_t, *, tm, tk, q_scale, wt_a_lo=None):
    b, s, d = x.shape
    grid = (b, s // tm)
    precise_idx = wt_a_lo is not None
    out_shape = (
        jax.ShapeDtypeStruct((b, N_HEADS, 2 * HEAD_DIM, s), BF16),
        jax.ShapeDtypeStruct((b, s, ATTN_WIDTH), F32),
        jax.ShapeDtypeStruct((b, s, ATTN_WIDTH), BF16),
        jax.ShapeDtypeStruct((b, s, ATTN_WIDTH), F32),
        jax.ShapeDtypeStruct((b, s // tk, N_HEADS * _V_ROWS, tk), BF16),
        jax.ShapeDtypeStruct((b, IDX_HEADS * IDX_DIM, s), F32 if precise_idx else BF16),
        jax.ShapeDtypeStruct((b, s, IDX_DIM), F32),
        jax.ShapeDtypeStruct((b, s, IDX_DIM), BF16),
        jax.ShapeDtypeStruct((b, 8, s), F32),
    )
    out_specs = (
        pl.BlockSpec((1, N_HEADS, 2 * HEAD_DIM, tm), lambda bi, i: (bi, 0, 0, i)),
        pl.BlockSpec((1, tm, ATTN_WIDTH), lambda bi, i: (bi, i, 0)),
        pl.BlockSpec((1, tm, ATTN_WIDTH), lambda bi, i: (bi, i, 0)),
        pl.BlockSpec((1, tm, ATTN_WIDTH), lambda bi, i: (bi, i, 0)),
        pl.BlockSpec((1, tm // tk, N_HEADS * _V_ROWS, tk), lambda bi, i: (bi, i, 0, 0)),
        pl.BlockSpec((1, IDX_HEADS * IDX_DIM, tm), lambda bi, i: (bi, 0, i)),
        pl.BlockSpec((1, tm, IDX_DIM), lambda bi, i: (bi, i, 0)),
        pl.BlockSpec((1, tm, IDX_DIM), lambda bi, i: (bi, i, 0)),
        pl.BlockSpec((1, 8, tm), lambda bi, i: (bi, 0, i)),
    )
    weights = [wt_a] + ([wt_a_lo] if precise_idx else [])
    in_specs = (
        [pl.BlockSpec((1, tm, d), lambda bi, i: (bi, i, 0))]
        + [pl.BlockSpec((_ROWS_A, d), lambda bi, i: (0, 0)) for _ in weights]
        + [pl.BlockSpec((ROPE_HALF, tm), lambda bi, i: (0, i)),
           pl.BlockSpec((ROPE_HALF, tm), lambda bi, i: (0, i))])
    return pl.pallas_call(
        functools.partial(_proj_attn_kernel, tk=tk, q_scale=q_scale, precise_idx=precise_idx),
        grid=grid, in_specs=in_specs, out_specs=out_specs, out_shape=out_shape,
        compiler_params=_cparams(("parallel", "parallel")), name="proj_attn",
    )(x, *weights, cos_t, sin_t)


def _gmlp_kernel(x_ref, w_ref, g_ref, b_ref, ws_ref, bst_ref, gm_ref, vn_ref, *, single_row_chunks):
    xb = x_ref[0].astype(BF16)
    tm = xb.shape[0]
    u = jnp.dot(xb, w_ref[:, :GMLP_WIDTH], preferred_element_type=F32)
    vb = jnp.dot(xb, w_ref[:, GMLP_WIDTH:], preferred_element_type=F32)
    vn = _layer_norm(vb, g_ref[...], b_ref[...])
    vn_ref[0] = vn
    gw = GMLP_WIDTH // GMLP_GROUPS
    row = lax.broadcasted_iota(jnp.int32, (CHUNK, CHUNK), 0)
    col = lax.broadcasted_iota(jnp.int32, (CHUNK, CHUNK), 1)
    if single_row_chunks:
        for g in range(GMLP_GROUPS):
            w00 = ws_ref[g][0:1, 0:1].astype(BF16).astype(F32)
            vg = vn[:, g * gw:(g + 1) * gw].astype(BF16).astype(F32)
            mix = vg * w00 + bst_ref[0:1, g:g + 1]
            gm_ref[0, :, g * gw:(g + 1) * gw] = (u[:, g * gw:(g + 1) * gw] * mix).astype(BF16)
    else:
        vnb = vn.astype(BF16)
        for g in range(GMLP_GROUPS):
            wsg = jnp.where(row >= col, ws_ref[g], 0.0).astype(BF16)
            bias = bst_ref[:, g:g + 1]
            for c in range(tm // CHUNK):
                mix = jnp.dot(wsg, vnb[c * CHUNK:(c + 1) * CHUNK, g * gw:(g + 1) * gw],
                              preferred_element_type=F32) + bias
                gm_ref[0, c * CHUNK:(c + 1) * CHUNK, g * gw:(g + 1) * gw] = (
                    u[c * CHUNK:(c + 1) * CHUNK, g * gw:(g + 1) * gw] * mix).astype(BF16)


def _gmlp(x, w_uv, ln_g, ln_b, w_spatial, b_spatial_t, *, tm, single_row_chunks):
    b, s, d = x.shape
    return pl.pallas_call(
        functools.partial(_gmlp_kernel, single_row_chunks=single_row_chunks),
        grid=(b, s // tm),
        in_specs=[
            pl.BlockSpec((1, tm, d), lambda bi, i: (bi, i, 0)),
            pl.BlockSpec((d, 2 * GMLP_WIDTH), lambda bi, i: (0, 0)),
            pl.BlockSpec((1, GMLP_WIDTH), lambda bi, i: (0, 0)),
            pl.BlockSpec((1, GMLP_WIDTH), lambda bi, i: (0, 0)),
            pl.BlockSpec((GMLP_GROUPS, CHUNK, CHUNK), lambda bi, i: (0, 0, 0)),
            pl.BlockSpec((CHUNK, GMLP_GROUPS), lambda bi, i: (0, 0)),
        ],
        out_specs=(pl.BlockSpec((1, tm, GMLP_WIDTH), lambda bi, i: (bi, i, 0)),
                   pl.BlockSpec((1, tm, GMLP_WIDTH), lambda bi, i: (bi, i, 0))),
        out_shape=(jax.ShapeDtypeStruct((b, s, GMLP_WIDTH), BF16),
                   jax.ShapeDtypeStruct((b, s, GMLP_WIDTH), F32)),
        compiler_params=_cparams(("parallel", "parallel")), name="gmlp",
    )(x, w_uv, ln_g, ln_b, w_spatial, b_spatial_t)


def _select_threshold(count_ge, count_gt, smin, smax, n_valid):
    k = float(TOPK)
    one = jnp.ones_like(smin)
    zero = jnp.zeros_like(smin)
    flag = lambda cnd: jnp.where(cnd, one, zero)
    live = flag(n_valid > k)
    lo = smin
    hi = smax
    top_tied = live * flag(count_ge(smax) >= k)
    lo = jnp.where(top_tied > 0.0, smax, lo)
    live = live * (1.0 - top_tied)
    c0 = count_ge(zero)
    c0p = count_gt(zero)
    ge0 = flag(c0 >= k)
    at_zero = live * ge0 * flag(c0p < k)
    lo = jnp.where(live * ge0 > 0.0, jnp.maximum(lo, 0.0), lo)
    hi = jnp.where(live * (1.0 - ge0) > 0.0, jnp.minimum(hi, 0.0), hi)
    live = live * (1.0 - at_zero)

    def cond(st):
        return jnp.sum(st[2]) > 0.0

    def body(st):
        lo, hi, live = st
        mid = 0.5 * lo + 0.5 * hi
        upd = live * flag(jnp.logical_and(mid > lo, mid < hi))
        c = count_ge(mid)
        ge = flag(c >= k)
        lo = jnp.where(upd * ge > 0.0, mid, lo)
        hi = jnp.where(upd * (1.0 - ge) > 0.0, mid, hi)
        live = upd * flag(c != k)
        return lo, hi, live

    lo, hi, live = lax.while_loop(cond, body, (lo, hi, live))
    m = k - count_gt(lo)
    return lo, m


def _prompt_attn_kernel(qz_ref, knb_ref, vtb_ref, qit_ref, kinb_ref, wit_ref, out_ref,
                        sc_ref, acc_ref, *, tq):
    tk = tq
    i = pl.program_id(1)
    nkc = i + 1
    wi = wit_ref[0]
    inf = jnp.float32(jnp.inf)

    def raw_scores(c):
        r0 = pl.multiple_of(c * tk, tk)
        kic = kinb_ref[0, pl.ds(r0, tk), :]
        s = None
        for h in range(IDX_HEADS):
            d = jnp.dot(kic, qit_ref[0, h * IDX_DIM:(h + 1) * IDX_DIM, :], preferred_element_type=F32)
            t = jnp.maximum(d, 0.0) * wi[h:h + 1, :]
            s = t if s is None else s + t
        return r0, s

    def score_chunk(c, carry):
        smin, smax = carry
        r0, s = raw_scores(c)
        sc_ref[pl.ds(r0, tk), :] = s
        return (jnp.minimum(smin, jnp.min(s, axis=0, keepdims=True)),
                jnp.maximum(smax, jnp.max(s, axis=0, keepdims=True)))

    smin, smax = lax.fori_loop(0, i, score_chunk,
                               (jnp.full((1, tq), inf, F32), jnp.full((1, tq), -inf, F32)))
    r0, s = raw_scores(i)
    vis = lax.broadcasted_iota(jnp.int32, (tk, tq), 0) <= lax.broadcasted_iota(jnp.int32, (tk, tq), 1)
    s_vis = jnp.where(vis, s, -inf)
    sc_ref[pl.ds(r0, tk), :] = s_vis
    smax = jnp.maximum(smax, jnp.max(s_vis, axis=0, keepdims=True))
    smin = jnp.minimum(smin, jnp.min(jnp.where(vis, s, inf), axis=0, keepdims=True))

    def make_count(strict):
        def count(v):
            def body(c, acc):
                r0 = pl.multiple_of(c * tk, tk)
                blk = sc_ref[pl.ds(r0, tk), :]
                hit = (blk > v) if strict else (blk >= v)
                ind = jnp.where(hit, 1.0, 0.0)
                return acc + jnp.sum(ind.reshape(tk // 8, 8, tq), axis=0)
            acc = lax.fori_loop(0, nkc, body, jnp.zeros((8, tq), F32))
            return jnp.sum(acc, axis=0, keepdims=True)
        return count

    n_valid = (i * tq + lax.broadcasted_iota(jnp.int32, (1, tq), 1) + 1).astype(F32)
    thr, m_take = _select_threshold(make_count(False), make_count(True), smin, smax, n_valid)

    acc_ref[...] = jnp.zeros_like(acc_ref)
    tri = (lax.broadcasted_iota(jnp.int32, (tk, tk), 0) >=
           lax.broadcasted_iota(jnp.int32, (tk, tk), 1)).astype(BF16)

    def attend_chunk(c, carry):
        tie_seen, ms, ls = carry
        r0 = pl.multiple_of(c * tk, tk)
        blk = sc_ref[pl.ds(r0, tk), :]
        tie = jnp.where(blk == thr, 1.0, 0.0).astype(BF16)
        rank = jnp.dot(tri, tie, preferred_element_type=F32) + tie_seen
        tie_rank = jnp.where(blk == thr, rank, inf)
        bias = jnp.where(blk > thr, 0.0, jnp.where(tie_rank <= m_take, 0.0, MASK_BIAS))
        kc = knb_ref[0, pl.ds(r0, tk), :]
        lgs = []
        for h in range(N_HEADS):
            k2 = kc[:, (h // 2) * 2 * HEAD_DIM:(h // 2 + 1) * 2 * HEAD_DIM]
            lgs.append(jnp.dot(k2, qz_ref[0, h], preferred_element_type=F32) + bias)
        new_ms = [jnp.maximum(ms[h], jnp.max(lgs[h], axis=0, keepdims=True)) for h in range(N_HEADS)]
        ps = [jnp.exp2(lgs[h] - new_ms[h]).astype(BF16) for h in range(N_HEADS)]
        pvs = [jnp.dot(vtb_ref[0, c, h * _V_ROWS:(h + 1) * _V_ROWS, :], ps[h], preferred_element_type=F32)
               for h in range(N_HEADS)]
        new_ls = []
        for h in range(N_HEADS):
            alpha = jnp.exp2(ms[h] - new_ms[h])
            acc_ref[h * HEAD_DIM:(h + 1) * HEAD_DIM, :] = (
                acc_ref[h * HEAD_DIM:(h + 1) * HEAD_DIM, :] * alpha + pvs[h][0:HEAD_DIM])
            new_ls.append(ls[h] * alpha + pvs[h][HEAD_DIM:HEAD_DIM + 1])
        return rank[tk - 1:tk, :], tuple(new_ms), tuple(new_ls)

    init = (jnp.zeros((1, tq), F32),
            tuple(jnp.full((1, tq), MASK_BIAS, F32) for _ in range(N_HEADS)),
            tuple(jnp.zeros((1, tq), F32) for _ in range(N_HEADS)))
    _, _, ls = lax.fori_loop(0, nkc, attend_chunk, init)

    out_t = jnp.concatenate(
        [acc_ref[h * HEAD_DIM:(h + 1) * HEAD_DIM, :] / ls[h] for h in range(N_HEADS)], axis=0)
    out_ref[0] = out_t.T.astype(BF16)


def _prompt_attn(qz, knb, vtb, qit, kinb, wit, *, tq):
    b, s, _ = knb.shape
    return pl.pallas_call(
        functools.partial(_prompt_attn_kernel, tq=tq),
        grid=(b, s // tq),
        in_specs=[
            pl.BlockSpec((1, N_HEADS, 2 * HEAD_DIM, tq), lambda bi, i: (bi, 0, 0, i)),
            pl.BlockSpec((1, s, ATTN_WIDTH), lambda bi, i: (bi, 0, 0), pipeline_mode=pl.Buffered(1)),
            pl.BlockSpec((1, s // tq, N_HEADS * _V_ROWS, tq), lambda bi, i: (bi, 0, 0, 0),
                         pipeline_mode=pl.Buffered(1)),
            pl.BlockSpec((1, IDX_HEADS * IDX_DIM, tq), lambda bi, i: (bi, 0, i)),
            pl.BlockSpec((1, s, IDX_DIM), lambda bi, i: (bi, 0, 0), pipeline_mode=pl.Buffered(1)),
            pl.BlockSpec((1, 8, tq), lambda bi, i: (bi, 0, i)),
        ],
        out_specs=pl.BlockSpec((1, tq, ATTN_WIDTH), lambda bi, i: (bi, i, 0)),
        out_shape=jax.ShapeDtypeStruct((b, s, ATTN_WIDTH), BF16),
        scratch_shapes=[
            pltpu.VMEM((s, tq), F32),
            pltpu.VMEM((ATTN_WIDTH, tq), F32),
        ],
        compiler_params=_cparams(("parallel", "arbitrary")), name="prompt_attn",
    )(qz, knb, vtb, qit, kinb, wit)


def _route(rl):
    lane = lax.broadcasted_iota(jnp.int32, rl.shape, 1).astype(F32)
    neg = jnp.float32(-jnp.inf)
    big = jnp.float32(1e9)
    g0 = float(N_EXPERTS)
    lg = jnp.where(jnp.logical_and(lane >= g0, lane < g0 + N_GROUPS), rl, neg)
    gmax = jnp.max(lg, axis=1, keepdims=True)
    gsel = jnp.min(jnp.where(lg == gmax, lane, big), axis=1, keepdims=True) - g0
    pgsel = 1.0 / jnp.sum(jnp.exp(lg - gmax), axis=1, keepdims=True)
    e0 = gsel * EXPERTS_PER_GROUP
    le = jnp.where(jnp.logical_and(lane >= e0, lane < e0 + EXPERTS_PER_GROUP), rl, neg)
    v1 = jnp.max(le, axis=1, keepdims=True)
    i1 = jnp.min(jnp.where(le == v1, lane, big), axis=1, keepdims=True)
    le2 = jnp.where(lane == i1, neg, le)
    v2 = jnp.max(le2, axis=1, keepdims=True)
    i2 = jnp.min(jnp.where(le2 == v2, lane, big), axis=1, keepdims=True)
    ex = jnp.exp(v2 - v1)
    den = 1.0 + ex
    p1 = pgsel / den
    p2 = pgsel * ex / den
    return jnp.where(lane == i1, p1, jnp.where(lane == i2, p2, 0.0))


def _merge_kernel(x_ref, attn_ref, gm_ref, wgate_ref, gbias_ref, wa_ref, wb_ref, wo_ref,
                  lng_ref, lnb_ref, wr_ref, br_ref, h_ref, comb_ref, *, precise_router):
    x = x_ref[0]
    xb = x.astype(BF16)
    ga = jax.nn.sigmoid(jnp.dot(xb, wgate_ref[:, :D_MODEL], preferred_element_type=F32) + gbias_ref[0:1, :])
    gb = jax.nn.sigmoid(jnp.dot(xb, wgate_ref[:, D_MODEL:], preferred_element_type=F32) + gbias_ref[1:2, :])
    a = jnp.dot(attn_ref[0], wa_ref[...], preferred_element_type=F32)
    bm = jnp.dot(gm_ref[0], wb_ref[...], preferred_element_type=F32)
    merged = ga * a + gb * bm
    y1 = DEEPNORM_ALPHA * x + jnp.dot(merged.astype(BF16), wo_ref[...], preferred_element_type=F32)
    h = _layer_norm(y1, lng_ref[...], lnb_ref[...])
    h_ref[0] = h
    if precise_router:
        h_hi, h_lo = _split_bf16(h)
        w_hi, w_lo = _split_bf16(wr_ref[...])
        rl = (jnp.dot(h_hi, w_hi, preferred_element_type=F32) + jnp.dot(h_lo, w_hi, preferred_element_type=F32)
              + jnp.dot(h_hi, w_lo, preferred_element_type=F32))
    else:
        rl = jnp.dot(h.astype(BF16), wr_ref[...].astype(BF16), preferred_element_type=F32)
    comb_ref[0] = _route(rl + br_ref[...])


def _merge(x, attn, gm, p, *, tm, precise_router):
    b, s, d = x.shape
    const = lambda shape: pl.BlockSpec(shape, lambda bi, i: tuple(0 for _ in shape))
    return pl.pallas_call(
        functools.partial(_merge_kernel, precise_router=precise_router),
        grid=(b, s // tm),
        in_specs=[
            pl.BlockSpec((1, tm, d), lambda bi, i: (bi, i, 0)),
            pl.BlockSpec((1, tm, ATTN_WIDTH), lambda bi, i: (bi, i, 0)),
            pl.BlockSpec((1, tm, GMLP_WIDTH), lambda bi, i: (bi, i, 0)),
            const((d, 2 * d)), const((2, d)), const((ATTN_WIDTH, d)), const((GMLP_WIDTH, d)), const((d, d)),
            const((1, d)), const((1, d)), const((d, LANES)), const((1, LANES)),
        ],
        out_specs=(pl.BlockSpec((1, tm, d), lambda bi, i: (bi, i, 0)),
                   pl.BlockSpec((1, tm, LANES), lambda bi, i: (bi, i, 0))),
        out_shape=(jax.ShapeDtypeStruct((b, s, d), F32), jax.ShapeDtypeStruct((b, s, LANES), F32)),
        compiler_params=_cparams(("parallel", "parallel")), name="merge_ln1_route",
    )(x, attn, gm, p["w_gates"], p["gate_bias"], p["w_branch_a"], p["w_branch_b"], p["w_o"],
      p["ln1_g"], p["ln1_b"], p["w_router"], p["b_router"])


def _moe_kernel(h_ref, comb_ref, wg_ref, wu_ref, wd_ref, lng_ref, lnb_ref, y_ref, acc_ref, hb_ref):
    e = pl.program_id(1)

    @pl.when(e == 0)
    def _():
        acc_ref[...] = jnp.zeros_like(acc_ref)
        hb_ref[...] = h_ref[...].astype(BF16)

    comb = comb_ref[...]
    lane = lax.broadcasted_iota(jnp.int32, comb.shape, 1)
    wcol = jnp.sum(jnp.where(lane == e, comb, 0.0), axis=1, keepdims=True)
    tm = hb_ref.shape[0]
    halves = [slice(r, r + tm // 2) for r in (0, tm // 2)] if tm % 16 == 0 else [slice(0, tm)]
    gates = [jnp.dot(hb_ref[rows, :], wg_ref[0], preferred_element_type=F32) for rows in halves]
    ups = [jnp.dot(hb_ref[rows, :], wu_ref[0], preferred_element_type=F32) for rows in halves]
    for rows, gate, up in zip(halves, gates, ups):
        hg = jax.nn.silu(gate) * up * wcol[rows]
        acc_ref[rows, :] += jnp.dot(hg.astype(BF16), wd_ref[0], preferred_element_type=F32)

    @pl.when(e == pl.num_programs(1) - 1)
    def _():
        y_ref[...] = _layer_norm(DEEPNORM_ALPHA * h_ref[...] + acc_ref[...], lng_ref[...], lnb_ref[...])


def _moe(h, comb, p, *, tm):
    t, d = h.shape
    return pl.pallas_call(
        _moe_kernel,
        grid=(t // tm, N_EXPERTS),
        in_specs=[
            pl.BlockSpec((tm, d), lambda i, e: (i, 0)),
            pl.BlockSpec((tm, LANES), lambda i, e: (i, 0)),
            pl.BlockSpec((1, d, EXPERT_DIM), lambda i, e: (e, 0, 0)),
            pl.BlockSpec((1, d, EXPERT_DIM), lambda i, e: (e, 0, 0)),
            pl.BlockSpec((1, EXPERT_DIM, d), lambda i, e: (e, 0, 0)),
            pl.BlockSpec((1, d), lambda i, e: (0, 0)),
            pl.BlockSpec((1, d), lambda i, e: (0, 0)),
        ],
        out_specs=pl.BlockSpec((tm, d), lambda i, e: (i, 0)),
        out_shape=jax.ShapeDtypeStruct((t, d), F32),
        scratch_shapes=[pltpu.VMEM((tm, d), F32), pltpu.VMEM((tm, d), BF16)],
        compiler_params=_cparams(("parallel", "arbitrary")), name="moe_ln2",
    )(h, comb, p["w_expert_gate"], p["w_expert_up"], p["w_expert_down"], p["ln2_g"], p["ln2_b"])


_PAGES_PER_STEP = 8


def _page_specs(block, n):
    def spec(r):
        return pl.BlockSpec(block, lambda b, j, pt: (pt[b, j * n + r], 0, 0))
    return [spec(r) for r in range(n)]


def _sample_scores_kernel(pt_ref, qi_ref, wi_ref, *refs):
    page_refs, out_ref = refs[:-1], refs[-1]
    nt = (((1,), (1,)), ((), ()))
    q_hi, q_lo = _split_bf16(qi_ref[0])
    q2 = jnp.concatenate([q_hi, q_lo], axis=0)
    wi = wi_ref[0][:, 0:1]
    pages = page_refs[0][0] if len(page_refs) == 1 else jnp.concatenate([p[0] for p in page_refs], axis=0)
    k_hi, k_lo = _split_bf16(pages)
    d2 = lax.dot_general(q2, k_hi, nt, preferred_element_type=F32)
    d = d2[0:8] + d2[8:16] + lax.dot_general(q_hi, k_lo, nt, preferred_element_type=F32)
    s = jnp.sum(jnp.maximum(d, 0.0) * wi, axis=0, keepdims=True)
    for r in range(len(page_refs)):
        out_ref[0, r:r + 1, :] = s[:, r * PAGE_SIZE:(r + 1) * PAGE_SIZE]


def _sample_scores(page_table, qi_s, wi_s, cache_kidx, *, n):
    db, n_pages = page_table.shape
    return pl.pallas_call(
        _sample_scores_kernel,
        grid_spec=pltpu.PrefetchScalarGridSpec(
            num_scalar_prefetch=1, grid=(db, n_pages // n),
            in_specs=[pl.BlockSpec((1, 8, IDX_DIM), lambda b, j, pt: (b, 0, 0)),
                      pl.BlockSpec((1, 8, LANES), lambda b, j, pt: (b, 0, 0))]
            + _page_specs((1, PAGE_SIZE, IDX_DIM), n),
            out_specs=pl.BlockSpec((1, n, PAGE_SIZE), lambda b, j, pt: (b, j, 0)),
        ),
        out_shape=jax.ShapeDtypeStruct((db, n_pages, PAGE_SIZE), F32),
        compiler_params=_cparams(("parallel", "arbitrary")), name="sample_scores",
    )(page_table, qi_s, wi_s, *([cache_kidx] * n))


def _sample_select_kernel(sc_ref, nvalid_ref, sel_ref, *, tk):
    nk, tq = sc_ref.shape
    nkc = nk // tk

    def chunk_reduce(fn, init):
        def body(c, acc):
            r0 = pl.multiple_of(c * tk, tk)
            return fn(acc, sc_ref[pl.ds(r0, tk), :])
        return lax.fori_loop(0, nkc, body, init)

    inf = jnp.float32(jnp.inf)
    smax = chunk_reduce(lambda a, blk: jnp.maximum(a, jnp.max(blk, axis=0, keepdims=True)),
                        jnp.full((1, tq), -inf, F32))
    smin = chunk_reduce(
        lambda a, blk: jnp.minimum(a, jnp.min(jnp.where(blk == -inf, inf, blk), axis=0, keepdims=True)),
        jnp.full((1, tq), inf, F32))

    def make_count(strict):
        def count(v):
            def fn(acc, blk):
                ind = jnp.where((blk > v) if strict else (blk >= v), 1.0, 0.0)
                return acc + jnp.sum(ind.reshape(tk // 8, 8, tq), axis=0)
            return jnp.sum(chunk_reduce(fn, jnp.zeros((8, tq), F32)), axis=0, keepdims=True)
        return count

    thr, m_take = _select_threshold(make_count(False), make_count(True), smin, smax, nvalid_ref[...])
    tri = (lax.broadcasted_iota(jnp.int32, (tk, tk), 0) >=
           lax.broadcasted_iota(jnp.int32, (tk, tk), 1)).astype(BF16)

    def emit(c, tie_seen):
        r0 = pl.multiple_of(c * tk, tk)
        blk = sc_ref[pl.ds(r0, tk), :]
        tie = jnp.where(blk == thr, 1.0, 0.0)
        rank = jnp.dot(tri, tie.astype(BF16), preferred_element_type=F32) + tie_seen
        keep_tie = tie * jnp.where(rank <= m_take, 1.0, 0.0)
        sel_ref[pl.ds(r0, tk), :] = jnp.where(blk > thr, 1.0, keep_tie)
        return tie_seen + jnp.sum(tie, axis=0, keepdims=True)

    lax.fori_loop(0, nkc, emit, jnp.zeros((1, tq), F32))


def _sample_select(sc_t, n_valid, *, tk):
    nk, tq = sc_t.shape
    return pl.pallas_call(
        functools.partial(_sample_select_kernel, tk=tk),
        grid=(1,),
        in_specs=[pl.BlockSpec((nk, tq), lambda i: (0, 0)), pl.BlockSpec((1, tq), lambda i: (0, 0))],
        out_specs=pl.BlockSpec((nk, tq), lambda i: (0, 0)),
        out_shape=jax.ShapeDtypeStruct((nk, tq), F32),
        compiler_params=_cparams(("arbitrary",)), name="sample_select",
    )(sc_t, n_valid)


def _sample_attend_kernel(pt_ref, qbd_ref, sel_ref, knew_ref, vnew_ref, selnew_ref, *refs,
                          n):
    k_refs, v_refs = refs[:n], refs[n:2 * n]
    out_ref, acc_ref, m_ref, l_ref = refs[2 * n:]
    j = pl.program_id(1)
    qbd = qbd_ref[0]

    def update(lgs, vals):
        m_old = m_ref[...]
        m_new = m_old
        for lg in lgs:
            m_new = jnp.maximum(m_new, jnp.max(lg, axis=1, keepdims=True))
        alpha = jnp.exp(m_old - m_new)
        l_new = alpha * l_ref[...]
        acc = acc_ref[...] * alpha
        for lg, val in zip(lgs, vals):
            p = jnp.exp(lg - m_new)
            l_new = l_new + jnp.sum(p, axis=1, keepdims=True)
            acc = acc + jnp.dot(p.astype(BF16), val, preferred_element_type=F32)
        l_ref[...] = l_new
        m_ref[...] = m_new
        acc_ref[...] = acc

    @pl.when(j == 0)
    def _():
        m_ref[...] = jnp.full_like(m_ref, MASK_BIAS)
        l_ref[...] = jnp.zeros_like(l_ref)
        acc_ref[...] = jnp.zeros_like(acc_ref)
        kb = knew_ref[0].astype(BF16)
        lg = lax.dot_general(qbd, kb, (((1,), (1,)), ((), ())), preferred_element_type=F32)
        col = lax.broadcasted_iota(jnp.int32, lg.shape, 1)
        keep = jnp.logical_and(col == 0, selnew_ref[0][:, 0:lg.shape[1]] > 0.0)
        update([jnp.where(keep, lg, MASK_BIAS)], [vnew_ref[0].astype(BF16)])

    lgs = []
    for r in range(n):
        kb = k_refs[r][0].astype(BF16)
        lg = lax.dot_general(qbd, kb, (((1,), (1,)), ((), ())), preferred_element_type=F32)
        lgs.append(lg + jnp.where(sel_ref[0, r:r + 1, :] > 0.0, 0.0, MASK_BIAS))
    update(lgs, [v_refs[r][0].astype(BF16) for r in range(n)])

    @pl.when(j == pl.num_programs(1) - 1)
    def _():
        row = lax.broadcasted_iota(jnp.int32, acc_ref.shape, 0)
        col = lax.broadcasted_iota(jnp.int32, acc_ref.shape, 1)
        own = (col // HEAD_DIM) == row
        out_ref[0] = jnp.sum(jnp.where(own, acc_ref[...] / l_ref[...], 0.0), axis=0, keepdims=True)


def _sample_attend(page_table, qbd, sel, k_new, v_new, sel_new, cache_k, cache_v):
    db, n_pages = page_table.shape
    n = _PAGES_PER_STEP
    per_sample = lambda shape: pl.BlockSpec(shape, lambda b, j, pt: (b, 0, 0))
    return pl.pallas_call(
        functools.partial(_sample_attend_kernel, n=n),
        grid_spec=pltpu.PrefetchScalarGridSpec(
            num_scalar_prefetch=1, grid=(db, n_pages // n),
            in_specs=[per_sample((1, 8, ATTN_WIDTH)),
                      pl.BlockSpec((1, n, PAGE_SIZE), lambda b, j, pt: (b, j, 0)),
                      per_sample((1, 16, ATTN_WIDTH)), per_sample((1, 16, ATTN_WIDTH)), per_sample((1, 8, LANES))]
            + _page_specs((1, PAGE_SIZE, ATTN_WIDTH), n) + _page_specs((1, PAGE_SIZE, ATTN_WIDTH), n),
            out_specs=pl.BlockSpec((1, 1, ATTN_WIDTH), lambda b, j, pt: (b, 0, 0)),
            scratch_shapes=[pltpu.VMEM((8, ATTN_WIDTH), F32), pltpu.VMEM((8, 1), F32), pltpu.VMEM((8, 1), F32)],
        ),
        out_shape=jax.ShapeDtypeStruct((db, 1, ATTN_WIDTH), F32),
        compiler_params=_cparams(("parallel", "arbitrary")), name="sample_attend",
    )(page_table, qbd, sel, k_new, v_new, sel_new, *([cache_k] * n), *([cache_v] * n))


def _rope_tables(pos):
    inv_freq = jnp.power(jnp.float32(ROPE_THETA), -jnp.arange(0, ROPE_ROT, 2, dtype=F32) / ROPE_ROT)
    ang = pos.astype(F32)[:, None] * inv_freq[None, :]
    return jnp.cos(ang).T, jnp.sin(ang).T


def _prepare_weights(p):
    w_in = p["w_in"]
    sizes = [ATTN_WIDTH, ATTN_WIDTH, ATTN_WIDTH, IDX_HEADS * IDX_DIM, IDX_DIM, IDX_HEADS,
             GMLP_WIDTH, GMLP_WIDTH, D_MODEL, D_MODEL]
    offs = np.concatenate([[0], np.cumsum(sizes)])
    n_attn = int(offs[6])
    wt_a_f32 = jnp.pad(w_in[:, :n_attn].T, ((0, _ROWS_A - n_attn), (0, 0)))
    wt_a = wt_a_f32.astype(BF16)
    wt_a_lo = (wt_a_f32 - wt_a.astype(F32)).astype(BF16)
    w_uv = w_in[:, int(offs[6]):int(offs[8])].astype(BF16)
    w_gates = w_in[:, int(offs[8]):].astype(BF16)
    w_re = jnp.transpose(p["w_router_expert"], (1, 0, 2)).reshape(D_MODEL, N_EXPERTS)
    w_router = jnp.pad(jnp.concatenate([w_re, p["w_router_group"]], axis=1),
                       ((0, 0), (0, LANES - N_EXPERTS - N_GROUPS)))
    b_router = jnp.pad(jnp.concatenate([p["b_router_expert"].reshape(-1), p["b_router_group"]]),
                       (0, LANES - N_EXPERTS - N_GROUPS)).reshape(1, LANES)
    return {
        "wt_a": wt_a, "wt_a_lo": wt_a_lo, "w_uv": w_uv, "w_gates": w_gates,
        "gate_bias": p["gate_bias"],
        "ln_v_g": p["ln_v_g"].reshape(1, -1), "ln_v_b": p["ln_v_b"].reshape(1, -1),
        "w_spatial": p["w_spatial"], "b_spatial_t": p["b_spatial"].T,
        "w_branch_a": p["w_branch_a"].astype(BF16), "w_branch_b": p["w_branch_b"].astype(BF16),
        "w_o": p["w_o"].astype(BF16),
        "ln1_g": p["ln1_g"].reshape(1, -1), "ln1_b": p["ln1_b"].reshape(1, -1),
        "ln2_g": p["ln2_g"].reshape(1, -1), "ln2_b": p["ln2_b"].reshape(1, -1),
        "w_router": w_router, "b_router": b_router,
        "w_expert_gate": p["w_expert_gate"].reshape(N_EXPERTS, D_MODEL, EXPERT_DIM).astype(BF16),
        "w_expert_up": p["w_expert_up"].reshape(N_EXPERTS, D_MODEL, EXPERT_DIM).astype(BF16),
        "w_expert_down": p["w_expert_down"].reshape(N_EXPERTS, EXPERT_DIM, D_MODEL).astype(BF16),
    }


def _pick_tile(n, pref):
    t = min(n, pref)
    assert n % t == 0, (n, pref)
    return t


def _channel_mix(x, attn, gm, p, *, precise_router):
    b, s, d = x.shape
    h, comb = _merge(x, attn, gm, p, tm=_pick_tile(s, 512), precise_router=precise_router)
    y = _moe(h.reshape(b * s, d), comb.reshape(b * s, LANES), p, tm=_pick_tile(b * s, 1024))
    return y.reshape(b, s, d)


def _prompt_step(x, p):
    b, s, _ = x.shape
    cos_t, sin_t = _rope_tables(jnp.arange(s, dtype=jnp.int32))
    tq = 256
    qz, kn, knb, vn, vtb, qit, kin, kinb, wit = _proj_attn(
        x, p["wt_a"], cos_t, sin_t, tm=512, tk=tq, q_scale=HEAD_DIM ** -0.5 * LOG2_E)
    attn = _prompt_attn(qz, knb, vtb, qit, kinb, wit, tq=tq)
    gm, _ = _gmlp(x, p["w_uv"], p["ln_v_g"], p["ln_v_b"], p["w_spatial"], p["b_spatial_t"],
                  tm=512, single_row_chunks=False)
    y = _channel_mix(x, attn, gm, p, precise_router=False)
    return y, kn.reshape(b, s, N_HEADS, HEAD_DIM), vn.reshape(b, s, N_HEADS, HEAD_DIM), kin


def _sample_step(x, cache_k, cache_v, cache_kidx, page_table, p):
    db, t_new, d = x.shape
    assert t_new == 1
    n_pages = page_table.shape[1]
    past = n_pages * PAGE_SIZE
    rows = LANES
    assert db <= rows
    xs = jnp.pad(x.reshape(1, db, d), ((0, 0), (0, rows - db), (0, 0)))
    cos_t, sin_t = _rope_tables(jnp.full((rows,), past, jnp.int32))
    qz, kn, knb, vn, vtb, qit, kin, kinb, wit = _proj_attn(
        xs, p["wt_a"], cos_t, sin_t, tm=rows, tk=rows, q_scale=HEAD_DIM ** -0.5, wt_a_lo=p["wt_a_lo"])
    gm, chunk_v = _gmlp(xs, p["w_uv"], p["ln_v_g"], p["ln_v_b"], p["w_spatial"], p["b_spatial_t"],
                        tm=rows, single_row_chunks=True)

    qi_s = jnp.pad(qit[0].T.reshape(rows, IDX_HEADS, IDX_DIM), ((0, 0), (0, 8 - IDX_HEADS), (0, 0)))[:db]
    wi_s = jnp.broadcast_to(wit[0].T[:db, :, None], (db, 8, LANES))
    sc_past = _sample_scores(page_table, qi_s, wi_s, cache_kidx, n=_PAGES_PER_STEP)
    own_page = jnp.pad(kin[0, :db][:, None, :], ((0, 0), (0, PAGE_SIZE - 1), (0, 0)))
    own_table = jnp.arange(db, dtype=jnp.int32).reshape(db, 1)
    sc_new = _sample_scores(own_table, qi_s, wi_s, own_page, n=1)[:, 0, 0]
    tk = 256
    nk = past + tk
    sc_all = jnp.concatenate([sc_past.reshape(db, past), sc_new[:, None],
                              jnp.full((db, tk - 1), -jnp.inf, F32)], axis=1)
    sc_t = jnp.pad(sc_all.T, ((0, 0), (0, rows - db)))
    n_valid = jnp.where(jnp.arange(rows) < db, float(past + 1), 0.0).astype(F32).reshape(1, rows)
    sel_t = _sample_select(sc_t, n_valid, tk=tk)
    sel = sel_t.T[:db]
    sel_past = sel[:, :past].reshape(db, n_pages, PAGE_SIZE)
    sel_new = jnp.broadcast_to(sel[:, past][:, None, None], (db, 8, LANES))

    q_pairs = jnp.transpose(qz[0][:, :, :db], (2, 0, 1))
    qbd = jnp.zeros((db, N_HEADS, N_HEADS // 2, 2 * HEAD_DIM), BF16)
    for hh in range(N_HEADS):
        qbd = qbd.at[:, hh, hh // 2, :].set(q_pairs[:, hh, :])
    qbd = qbd.reshape(db, N_HEADS, ATTN_WIDTH)
    k_new = jnp.pad(kn[0, :db][:, None, :], ((0, 0), (0, 15), (0, 0)))
    v_new = jnp.pad(vn[0, :db][:, None, :], ((0, 0), (0, 15), (0, 0)))
    attn = _sample_attend(page_table, qbd, sel_past, k_new, v_new, sel_new,
                          cache_k.reshape(cache_k.shape[0], PAGE_SIZE, ATTN_WIDTH),
                          cache_v.reshape(cache_v.shape[0], PAGE_SIZE, ATTN_WIDTH))
    attn_p = jnp.pad(attn.reshape(1, db, ATTN_WIDTH), ((0, 0), (0, rows - db), (0, 0))).astype(BF16)
    y = _channel_mix(xs, attn_p, gm, p, precise_router=True)
    return (y[0, :db].reshape(db, 1, d),
            kn[0, :db].reshape(db, 1, N_HEADS, HEAD_DIM), vn[0, :db].reshape(db, 1, N_HEADS, HEAD_DIM),
            kin[0, :db].reshape(db, 1, IDX_DIM), chunk_v[0, :db].reshape(db, 1, GMLP_WIDTH))


def kernel(x_prompt, x_sample, cache_k, cache_v, cache_kidx, page_table, w_in, gate_bias,
           ln_v_g, ln_v_b, w_spatial, b_spatial, w_branch_a, w_branch_b, w_o, ln1_g, ln1_b,
           w_router_group, b_router_group, w_router_expert, b_router_expert,
           w_expert_gate, w_expert_up, w_expert_down, ln2_g, ln2_b):
    p = _prepare_weights(dict(
        w_in=w_in, gate_bias=gate_bias, ln_v_g=ln_v_g, ln_v_b=ln_v_b, w_spatial=w_spatial,
        b_spatial=b_spatial, w_branch_a=w_branch_a, w_branch_b=w_branch_b, w_o=w_o, ln1_g=ln1_g,
        ln1_b=ln1_b, w_router_group=w_router_group, b_router_group=b_router_group,
        w_router_expert=w_router_expert, b_router_expert=b_router_expert,
        w_expert_gate=w_expert_gate, w_expert_up=w_expert_up, w_expert_down=w_expert_down,
        ln2_g=ln2_g, ln2_b=ln2_b))
    y_p, k_p, v_p, ki_p = _prompt_step(x_prompt, p)
    y_s, k_s, v_s, ki_s, cv_s = _sample_step(x_sample, cache_k, cache_v, cache_kidx, page_table, p)
    return (y_p, y_s, k_p, v_p, ki_p, k_s, v_s, ki_s, cv_s)
```

```python
import functools

import jax
import jax.numpy as jnp
import numpy as np
from jax import lax
from jax.experimental import pallas as pl
from jax.experimental.pallas import tpu as pltpu

F32 = jnp.float32
BF16 = jnp.bfloat16

D_MODEL = 1024
N_HEADS = 8
HEAD_DIM = 64
ATTN_WIDTH = N_HEADS * HEAD_DIM
IDX_HEADS = 4
IDX_DIM = 64
TOPK = 256
ROPE_THETA = 500000.0
ROPE_ROT = HEAD_DIM // 4
ROPE_HALF = ROPE_ROT // 2
CHUNK = 128
GMLP_GROUPS = 4
GMLP_WIDTH = D_MODEL // 2
N_GROUPS = 4
EXPERTS_PER_GROUP = 8
N_EXPERTS = N_GROUPS * EXPERTS_PER_GROUP
EXPERT_DIM = D_MODEL // 4
PAGE_SIZE = 128
LN_EPS = 1e-5
DEEPNORM_ALPHA = 2.0 ** 0.25
MASK_BIAS = -1e30

VMEM_LIMIT_V7X = 56 * 1024 * 1024
LANES = 128

_ROW_Q = 0
_ROW_K = _ROW_Q + ATTN_WIDTH
_ROW_V = _ROW_K + ATTN_WIDTH
_ROW_QI = _ROW_V + ATTN_WIDTH
_ROW_KI = _ROW_QI + IDX_HEADS * IDX_DIM
_ROW_WI = _ROW_KI + IDX_DIM
_WI_PAD = 16
_ROWS_A = _ROW_WI + _WI_PAD
_V_ROWS = HEAD_DIM + 16
LOG2_E = 1.4426950408889634


def _cparams(sem):
    return pltpu.CompilerParams(dimension_semantics=sem, vmem_limit_bytes=VMEM_LIMIT_V7X)


def _layer_norm(x, g, b):
    mu = jnp.mean(x, axis=-1, keepdims=True)
    xc = x - mu
    var = jnp.mean(xc * xc, axis=-1, keepdims=True)
    return xc * lax.rsqrt(var + LN_EPS) * g + b


def _rope_rows(seg, cos, sin):
    x1 = seg[0:ROPE_HALF]
    x2 = seg[ROPE_HALF:ROPE_ROT]
    return jnp.concatenate([x1 * cos - x2 * sin, x2 * cos + x1 * sin, seg[ROPE_ROT:]], axis=0)


def _split_bf16(a):
    hi = a.astype(BF16)
    return hi, (a - hi.astype(F32)).astype(BF16)


def _proj_attn_kernel(x_ref, wt_ref, *rest, tk, q_scale, precise_idx):
    if precise_idx:
        wtlo_ref, rest = rest[0], rest[1:]
    (cos_ref, sin_ref,
     qz_ref, kn_ref, knb_ref, vn_ref, vtb_ref, qit_ref, kin_ref, kinb_ref, wit_ref) = rest
    x = x_ref[0]
    xb = x.astype(BF16)
    tm = xb.shape[0]
    cos = cos_ref[...]
    sin = sin_ref[...]
    nt = (((1,), (1,)), ((), ()))

    def proj_t(r0, n):
        return lax.dot_general(wt_ref[r0:r0 + n, :], xb, nt, preferred_element_type=F32)

    if precise_idx:
        xlo = (x - xb.astype(F32)).astype(BF16)

        def proj_idx(r0, n):
            w_hi = wt_ref[r0:r0 + n, :]
            return (lax.dot_general(w_hi, xb, nt, preferred_element_type=F32)
                    + lax.dot_general(w_hi, xlo, nt, preferred_element_type=F32)
                    + lax.dot_general(wtlo_ref[r0:r0 + n, :], xb, nt, preferred_element_type=F32))
    else:
        proj_idx = proj_t

    q_t = proj_t(_ROW_Q, ATTN_WIDTH)
    zeros = jnp.zeros((HEAD_DIM, tm), F32)
    for h in range(N_HEADS):
        rot = _rope_rows(q_t[h * HEAD_DIM:(h + 1) * HEAD_DIM], cos, sin) * q_scale
        pair = [rot, zeros] if h % 2 == 0 else [zeros, rot]
        qz_ref[0, h] = jnp.concatenate(pair, axis=0).astype(BF16)

    k_t = proj_t(_ROW_K, ATTN_WIDTH)
    k_rot = jnp.concatenate(
        [_rope_rows(k_t[h * HEAD_DIM:(h + 1) * HEAD_DIM], cos, sin) for h in range(N_HEADS)], axis=0)
    kn_ref[0] = k_rot
    knb_ref[0] = k_rot.T.astype(BF16)

    v_t = proj_t(_ROW_V, ATTN_WIDTH)
    vn_ref[0] = v_t
    ones = jnp.ones((_V_ROWS - HEAD_DIM, tk), F32)
    for c in range(tm // tk):
        v_c = v_t[:, c * tk:(c + 1) * tk]
        vtb_ref[0, c] = jnp.concatenate(
            [piece for h in range(N_HEADS) for piece in (v_c[h * HEAD_DIM:(h + 1) * HEAD_DIM], ones)],
            axis=0).astype(BF16)

    qi_t = proj_idx(_ROW_QI, IDX_HEADS * IDX_DIM)
    qit_ref[0] = jnp.concatenate(
        [_rope_rows(qi_t[h * IDX_DIM:(h + 1) * IDX_DIM], cos, sin) for h in range(IDX_HEADS)],
        axis=0).astype(qit_ref.dtype)

    ki_t = _rope_rows(proj_idx(_ROW_KI, IDX_DIM), cos, sin)
    kin_ref[0] = ki_t
    kinb_ref[0] = ki_t.T.astype(BF16)

    wit_ref[0] = proj_idx(_ROW_WI, _WI_PAD)[0:8]


def _proj_attn(x, wt_a, cos_t, sin_t, *, tm, tk, q_scale, wt_a_lo=None):
    b, s, d = x.shape
    grid = (b, s // tm)
    precise_idx = wt_a_lo is not None
    out_shape = (
        jax.ShapeDtypeStruct((b, N_HEADS, 2 * HEAD_DIM, s), BF16),
        jax.ShapeDtypeStruct((b, ATTN_WIDTH, s), F32),
        jax.ShapeDtypeStruct((b, s, ATTN_WIDTH), BF16),
        jax.ShapeDtypeStruct((b, ATTN_WIDTH, s), F32),
        jax.ShapeDtypeStruct((b, s // tk, N_HEADS * _V_ROWS, tk), BF16),
        jax.ShapeDtypeStruct((b, IDX_HEADS * IDX_DIM, s), F32 if precise_idx else BF16),
        jax.ShapeDtypeStruct((b, IDX_DIM, s), F32),
        jax.ShapeDtypeStruct((b, s, IDX_DIM), BF16),
        jax.ShapeDtypeStruct((b, 8, s), F32),
    )
    out_specs = (
        pl.BlockSpec((1, N_HEADS, 2 * HEAD_DIM, tm), lambda bi, i: (bi, 0, 0, i)),
        pl.BlockSpec((1, ATTN_WIDTH, tm), lambda bi, i: (bi, 0, i)),
        pl.BlockSpec((1, tm, ATTN_WIDTH), lambda bi, i: (bi, i, 0)),
        pl.BlockSpec((1, ATTN_WIDTH, tm), lambda bi, i: (bi, 0, i)),
        pl.BlockSpec((1, tm // tk, N_HEADS * _V_ROWS, tk), lambda bi, i: (bi, i, 0, 0)),
        pl.BlockSpec((1, IDX_HEADS * IDX_DIM, tm), lambda bi, i: (bi, 0, i)),
        pl.BlockSpec((1, IDX_DIM, tm), lambda bi, i: (bi, 0, i)),
        pl.BlockSpec((1, tm, IDX_DIM), lambda bi, i: (bi, i, 0)),
        pl.BlockSpec((1, 8, tm), lambda bi, i: (bi, 0, i)),
    )
    weights = [wt_a] + ([wt_a_lo] if precise_idx else [])
    in_specs = (
        [pl.BlockSpec((1, tm, d), lambda bi, i: (bi, i, 0))]
        + [pl.BlockSpec((_ROWS_A, d), lambda bi, i: (0, 0)) for _ in weights]
        + [pl.BlockSpec((ROPE_HALF, tm), lambda bi, i: (0, i)),
           pl.BlockSpec((ROPE_HALF, tm), lambda bi, i: (0, i))])
    return pl.pallas_call(
        functools.partial(_proj_attn_kernel, tk=tk, q_scale=q_scale, precise_idx=precise_idx),
        grid=grid, in_specs=in_specs, out_specs=out_specs, out_shape=out_shape,
        compiler_params=_cparams(("parallel", "parallel")), name="proj_attn",
    )(x, *weights, cos_t, sin_t)


def _gmlp_kernel(x_ref, w_ref, g_ref, b_ref, ws_ref, bst_ref, gm_ref, vn_ref, *, single_row_chunks):
    xb = x_ref[0].astype(BF16)
    tm = xb.shape[0]
    u = jnp.dot(xb, w_ref[:, :GMLP_WIDTH], preferred_element_type=F32)
    vb = jnp.dot(xb, w_ref[:, GMLP_WIDTH:], preferred_element_type=F32)
    vn = _layer_norm(vb, g_ref[...], b_ref[...])
    vn_ref[0] = vn
    gw = GMLP_WIDTH // GMLP_GROUPS
    row = lax.broadcasted_iota(jnp.int32, (CHUNK, CHUNK), 0)
    col = lax.broadcasted_iota(jnp.int32, (CHUNK, CHUNK), 1)
    if single_row_chunks:
        for g in range(GMLP_GROUPS):
            w00 = ws_ref[g][0:1, 0:1].astype(BF16).astype(F32)
            vg = vn[:, g * gw:(g + 1) * gw].astype(BF16).astype(F32)
            mix = vg * w00 + bst_ref[0:1, g:g + 1]
            gm_ref[0, :, g * gw:(g + 1) * gw] = (u[:, g * gw:(g + 1) * gw] * mix).astype(BF16)
    else:
        vnb = vn.astype(BF16)
        for g in range(GMLP_GROUPS):
            wsg = jnp.where(row >= col, ws_ref[g], 0.0).astype(BF16)
            bias = bst_ref[:, g:g + 1]
            for c in range(tm // CHUNK):
                mix = jnp.dot(wsg, vnb[c * CHUNK:(c + 1) * CHUNK, g * gw:(g + 1) * gw],
                              preferred_element_type=F32) + bias
                gm_ref[0, c * CHUNK:(c + 1) * CHUNK, g * gw:(g + 1) * gw] = (
                    u[c * CHUNK:(c + 1) * CHUNK, g * gw:(g + 1) * gw] * mix).astype(BF16)


def _gmlp(x, w_uv, ln_g, ln_b, w_spatial, b_spatial_t, *, tm, single_row_chunks):
    b, s, d = x.shape
    return pl.pallas_call(
        functools.partial(_gmlp_kernel, single_row_chunks=single_row_chunks),
        grid=(b, s // tm),
        in_specs=[
            pl.BlockSpec((1, tm, d), lambda bi, i: (bi, i, 0)),
            pl.BlockSpec((d, 2 * GMLP_WIDTH), lambda bi, i: (0, 0)),
            pl.BlockSpec((1, GMLP_WIDTH), lambda bi, i: (0, 0)),
            pl.BlockSpec((1, GMLP_WIDTH), lambda bi, i: (0, 0)),
            pl.BlockSpec((GMLP_GROUPS, CHUNK, CHUNK), lambda bi, i: (0, 0, 0)),
            pl.BlockSpec((CHUNK, GMLP_GROUPS), lambda bi, i: (0, 0)),
        ],
        out_specs=(pl.BlockSpec((1, tm, GMLP_WIDTH), lambda bi, i: (bi, i, 0)),
                   pl.BlockSpec((1, tm, GMLP_WIDTH), lambda bi, i: (bi, i, 0))),
        out_shape=(jax.ShapeDtypeStruct((b, s, GMLP_WIDTH), BF16),
                   jax.ShapeDtypeStruct((b, s, GMLP_WIDTH), F32)),
        compiler_params=_cparams(("parallel", "parallel")), name="gmlp",
    )(x, w_uv, ln_g, ln_b, w_spatial, b_spatial_t)


def _select_threshold(count_ge, count_gt, smin, smax, n_valid):
    k = float(TOPK)
    one = jnp.ones_like(smin)
    zero = jnp.zeros_like(smin)
    flag = lambda cnd: jnp.where(cnd, one, zero)
    live = flag(n_valid > k)
    lo = smin
    hi = smax
    top_tied = live * flag(count_ge(smax) >= k)
    lo = jnp.where(top_tied > 0.0, smax, lo)
    live = live * (1.0 - top_tied)
    c0 = count_ge(zero)
    c0p = count_gt(zero)
    ge0 = flag(c0 >= k)
    at_zero = live * ge0 * flag(c0p < k)
    lo = jnp.where(live * ge0 > 0.0, jnp.maximum(lo, 0.0), lo)
    hi = jnp.where(live * (1.0 - ge0) > 0.0, jnp.minimum(hi, 0.0), hi)
    live = live * (1.0 - at_zero)

    def cond(st):
        return jnp.sum(st[2]) > 0.0

    def body(st):
        lo, hi, live = st
        mid = 0.5 * lo + 0.5 * hi
        upd = live * flag(jnp.logical_and(mid > lo, mid < hi))
        c = count_ge(mid)
        ge = flag(c >= k)
        lo = jnp.where(upd * ge > 0.0, mid, lo)
        hi = jnp.where(upd * (1.0 - ge) > 0.0, mid, hi)
        live = upd * flag(c != k)
        return lo, hi, live

    lo, hi, live = lax.while_loop(cond, body, (lo, hi, live))
    m = k - count_gt(lo)
    return lo, m


def _prompt_attn_kernel(qz_ref, knb_ref, vtb_ref, qit_ref, kinb_ref, wit_ref, out_ref,
                        sc_ref, acc_ref, *, tq):
    tk = tq
    i = pl.program_id(1)
    nkc = i + 1
    wi = wit_ref[0]
    inf = jnp.float32(jnp.inf)

    def raw_scores(c):
        r0 = pl.multiple_of(c * tk, tk)
        kic = kinb_ref[0, pl.ds(r0, tk), :]
        s = None
        for h in range(IDX_HEADS):
            d = jnp.dot(kic, qit_ref[0, h * IDX_DIM:(h + 1) * IDX_DIM, :], preferred_element_type=F32)
            t = jnp.maximum(d, 0.0) * wi[h:h + 1, :]
            s = t if s is None else s + t
        return r0, s

    def score_chunk(c, carry):
        smin, smax = carry
        r0, s = raw_scores(c)
        sc_ref[pl.ds(r0, tk), :] = s
        return (jnp.minimum(smin, jnp.min(s, axis=0, keepdims=True)),
                jnp.maximum(smax, jnp.max(s, axis=0, keepdims=True)))

    smin, smax = lax.fori_loop(0, i, score_chunk,
                               (jnp.full((1, tq), inf, F32), jnp.full((1, tq), -inf, F32)))
    r0, s = raw_scores(i)
    vis = lax.broadcasted_iota(jnp.int32, (tk, tq), 0) <= lax.broadcasted_iota(jnp.int32, (tk, tq), 1)
    s_vis = jnp.where(vis, s, -inf)
    sc_ref[pl.ds(r0, tk), :] = s_vis
    smax = jnp.maximum(smax, jnp.max(s_vis, axis=0, keepdims=True))
    smin = jnp.minimum(smin, jnp.min(jnp.where(vis, s, inf), axis=0, keepdims=True))

    def make_count(strict):
        def count(v):
            def body(c, acc):
                r0 = pl.multiple_of(c * tk, tk)
                blk = sc_ref[pl.ds(r0, tk), :]
                hit = (blk > v) if strict else (blk >= v)
                ind = jnp.where(hit, 1.0, 0.0)
                return acc + jnp.sum(ind.reshape(tk // 8, 8, tq), axis=0)
            acc = lax.fori_loop(0, nkc, body, jnp.zeros((8, tq), F32))
            return jnp.sum(acc, axis=0, keepdims=True)
        return count

    n_valid = (i * tq + lax.broadcasted_iota(jnp.int32, (1, tq), 1) + 1).astype(F32)
    thr, m_take = _select_threshold(make_count(False), make_count(True), smin, smax, n_valid)

    acc_ref[...] = jnp.zeros_like(acc_ref)
    tri = (lax.broadcasted_iota(jnp.int32, (tk, tk), 0) >=
           lax.broadcasted_iota(jnp.int32, (tk, tk), 1)).astype(BF16)

    def attend_chunk(c, carry):
        tie_seen, ms, ls = carry
        r0 = pl.multiple_of(c * tk, tk)
        blk = sc_ref[pl.ds(r0, tk), :]
        tie = jnp.where(blk == thr, 1.0, 0.0).astype(BF16)
        rank = jnp.dot(tri, tie, preferred_element_type=F32) + tie_seen
        tie_rank = jnp.where(blk == thr, rank, inf)
        bias = jnp.where(blk > thr, 0.0, jnp.where(tie_rank <= m_take, 0.0, MASK_BIAS))
        kc = knb_ref[0, pl.ds(r0, tk), :]
        lgs = []
        for h in range(N_HEADS):
            k2 = kc[:, (h // 2) * 2 * HEAD_DIM:(h // 2 + 1) * 2 * HEAD_DIM]
            lgs.append(jnp.dot(k2, qz_ref[0, h], preferred_element_type=F32) + bias)
        new_ms = [jnp.maximum(ms[h], jnp.max(lgs[h], axis=0, keepdims=True)) for h in range(N_HEADS)]
        ps = [jnp.exp2(lgs[h] - new_ms[h]).astype(BF16) for h in range(N_HEADS)]
        pvs = [jnp.dot(vtb_ref[0, c, h * _V_ROWS:(h + 1) * _V_ROWS, :], ps[h], preferred_element_type=F32)
               for h in range(N_HEADS)]
        new_ls = []
        for h in range(N_HEADS):
            alpha = jnp.exp2(ms[h] - new_ms[h])
            acc_ref[h * HEAD_DIM:(h + 1) * HEAD_DIM, :] = (
                acc_ref[h * HEAD_DIM:(h + 1) * HEAD_DIM, :] * alpha + pvs[h][0:HEAD_DIM])
            new_ls.append(ls[h] * alpha + pvs[h][HEAD_DIM:HEAD_DIM + 1])
        return rank[tk - 1:tk, :], tuple(new_ms), tuple(new_ls)

    init = (jnp.zeros((1, tq), F32),
            tuple(jnp.full((1, tq), MASK_BIAS, F32) for _ in range(N_HEADS)),
            tuple(jnp.zeros((1, tq), F32) for _ in range(N_HEADS)))
    _, _, ls = lax.fori_loop(0, nkc, attend_chunk, init)

    out_t = jnp.concatenate(
        [acc_ref[h * HEAD_DIM:(h + 1) * HEAD_DIM, :] / ls[h] for h in range(N_HEADS)], axis=0)
    out_ref[0] = out_t.T.astype(BF16)


def _prompt_attn(qz, knb, vtb, qit, kinb, wit, *, tq):
    b, s, _ = knb.shape
    return pl.pallas_call(
        functools.partial(_prompt_attn_kernel, tq=tq),
        grid=(b, s // tq),
        in_specs=[
            pl.BlockSpec((1, N_HEADS, 2 * HEAD_DIM, tq), lambda bi, i: (bi, 0, 0, i)),
            pl.BlockSpec((1, s, ATTN_WIDTH), lambda bi, i: (bi, 0, 0), pipeline_mode=pl.Buffered(1)),
            pl.BlockSpec((1, s // tq, N_HEADS * _V_ROWS, tq), lambda bi, i: (bi, 0, 0, 0),
                         pipeline_mode=pl.Buffered(1)),
            pl.BlockSpec((1, IDX_HEADS * IDX_DIM, tq), lambda bi, i: (bi, 0, i)),
            pl.BlockSpec((1, s, IDX_DIM), lambda bi, i: (bi, 0, 0), pipeline_mode=pl.Buffered(1)),
            pl.BlockSpec((1, 8, tq), lambda bi, i: (bi, 0, i)),
        ],
        out_specs=pl.BlockSpec((1, tq, ATTN_WIDTH), lambda bi, i: (bi, i, 0)),
        out_shape=jax.ShapeDtypeStruct((b, s, ATTN_WIDTH), BF16),
        scratch_shapes=[
            pltpu.VMEM((s, tq), F32),
            pltpu.VMEM((ATTN_WIDTH, tq), F32),
        ],
        compiler_params=_cparams(("parallel", "arbitrary")), name="prompt_attn",
    )(qz, knb, vtb, qit, kinb, wit)


def _route(rl):
    lane = lax.broadcasted_iota(jnp.int32, rl.shape, 1).astype(F32)
    neg = jnp.float32(-jnp.inf)
    big = jnp.float32(1e9)
    g0 = float(N_EXPERTS)
    lg = jnp.where(jnp.logical_and(lane >= g0, lane < g0 + N_GROUPS), rl, neg)
    gmax = jnp.max(lg, axis=1, keepdims=True)
    gsel = jnp.min(jnp.where(lg == gmax, lane, big), axis=1, keepdims=True) - g0
    pgsel = 1.0 / jnp.sum(jnp.exp(lg - gmax), axis=1, keepdims=True)
    e0 = gsel * EXPERTS_PER_GROUP
    le = jnp.where(jnp.logical_and(lane >= e0, lane < e0 + EXPERTS_PER_GROUP), rl, neg)
    v1 = jnp.max(le, axis=1, keepdims=True)
    i1 = jnp.min(jnp.where(le == v1, lane, big), axis=1, keepdims=True)
    le2 = jnp.where(lane == i1, neg, le)
    v2 = jnp.max(le2, axis=1, keepdims=True)
    i2 = jnp.min(jnp.where(le2 == v2, lane, big), axis=1, keepdims=True)
    ex = jnp.exp(v2 - v1)
    den = 1.0 + ex
    p1 = pgsel / den
    p2 = pgsel * ex / den
    return jnp.where(lane == i1, p1, jnp.where(lane == i2, p2, 0.0))


def _merge_kernel(x_ref, attn_ref, gm_ref, wgate_ref, gbias_ref, wa_ref, wb_ref, wo_ref,
                  lng_ref, lnb_ref, wr_ref, br_ref, h_ref, comb_ref, *, precise_router):
    x = x_ref[0]
    xb = x.astype(BF16)
    ga = jax.nn.sigmoid(jnp.dot(xb, wgate_ref[:, :D_MODEL], preferred_element_type=F32) + gbias_ref[0:1, :])
    gb = jax.nn.sigmoid(jnp.dot(xb, wgate_ref[:, D_MODEL:], preferred_element_type=F32) + gbias_ref[1:2, :])
    a = jnp.dot(attn_ref[0], wa_ref[...], preferred_element_type=F32)
    bm = jnp.dot(gm_ref[0], wb_ref[...], preferred_element_type=F32)
    merged = ga * a + gb * bm
    y1 = DEEPNORM_ALPHA * x + jnp.dot(merged.astype(BF16), wo_ref[...], preferred_element_type=F32)
    h = _layer_norm(y1, lng_ref[...], lnb_ref[...])
    h_ref[0] = h
    if precise_router:
        h_hi, h_lo = _split_bf16(h)
        w_hi, w_lo = _split_bf16(wr_ref[...])
        rl = (jnp.dot(h_hi, w_hi, preferred_element_type=F32) + jnp.dot(h_lo, w_hi, preferred_element_type=F32)
              + jnp.dot(h_hi, w_lo, preferred_element_type=F32))
    else:
        rl = jnp.dot(h.astype(BF16), wr_ref[...].astype(BF16), preferred_element_type=F32)
    comb_ref[0] = _route(rl + br_ref[...])


def _merge(x, attn, gm, p, *, tm, precise_router):
    b, s, d = x.shape
    const = lambda shape: pl.BlockSpec(shape, lambda bi, i: tuple(0 for _ in shape))
    return pl.pallas_call(
        functools.partial(_merge_kernel, precise_router=precise_router),
        grid=(b, s // tm),
        in_specs=[
            pl.BlockSpec((1, tm, d), lambda bi, i: (bi, i, 0)),
            pl.BlockSpec((1, tm, ATTN_WIDTH), lambda bi, i: (bi, i, 0)),
            pl.BlockSpec((1, tm, GMLP_WIDTH), lambda bi, i: (bi, i, 0)),
            const((d, 2 * d)), const((2, d)), const((ATTN_WIDTH, d)), const((GMLP_WIDTH, d)), const((d, d)),
            const((1, d)), const((1, d)), const((d, LANES)), const((1, LANES)),
        ],
        out_specs=(pl.BlockSpec((1, tm, d), lambda bi, i: (bi, i, 0)),
                   pl.BlockSpec((1, tm, LANES), lambda bi, i: (bi, i, 0))),
        out_shape=(jax.ShapeDtypeStruct((b, s, d), F32), jax.ShapeDtypeStruct((b, s, LANES), F32)),
        compiler_params=_cparams(("parallel", "parallel")), name="merge_ln1_route",
    )(x, attn, gm, p["w_gates"], p["gate_bias"], p["w_branch_a"], p["w_branch_b"], p["w_o"],
      p["ln1_g"], p["ln1_b"], p["w_router"], p["b_router"])


def _moe_kernel(h_ref, comb_ref, wg_ref, wu_ref, wd_ref, lng_ref, lnb_ref, y_ref, acc_ref, hb_ref):
    e = pl.program_id(1)

    @pl.when(e == 0)
    def _():
        acc_ref[...] = jnp.zeros_like(acc_ref)
        hb_ref[...] = h_ref[...].astype(BF16)

    comb = comb_ref[...]
    lane = lax.broadcasted_iota(jnp.int32, comb.shape, 1)
    wcol = jnp.sum(jnp.where(lane == e, comb, 0.0), axis=1, keepdims=True)
    tm = hb_ref.shape[0]
    halves = [slice(r, r + tm // 2) for r in (0, tm // 2)] if tm % 16 == 0 else [slice(0, tm)]
    gates = [jnp.dot(hb_ref[rows, :], wg_ref[0], preferred_element_type=F32) for rows in halves]
    ups = [jnp.dot(hb_ref[rows, :], wu_ref[0], preferred_element_type=F32) for rows in halves]
    for rows, gate, up in zip(halves, gates, ups):
        hg = jax.nn.silu(gate) * up * wcol[rows]
        acc_ref[rows, :] += jnp.dot(hg.astype(BF16), wd_ref[0], preferred_element_type=F32)

    @pl.when(e == pl.num_programs(1) - 1)
    def _():
        y_ref[...] = _layer_norm(DEEPNORM_ALPHA * h_ref[...] + acc_ref[...], lng_ref[...], lnb_ref[...])


def _moe(h, comb, p, *, tm):
    t, d = h.shape
    return pl.pallas_call(
        _moe_kernel,
        grid=(t // tm, N_EXPERTS),
        in_specs=[
            pl.BlockSpec((tm, d), lambda i, e: (i, 0)),
            pl.BlockSpec((tm, LANES), lambda i, e: (i, 0)),
            pl.BlockSpec((1, d, EXPERT_DIM), lambda i, e: (e, 0, 0)),
            pl.BlockSpec((1, d, EXPERT_DIM), lambda i, e: (e, 0, 0)),
            pl.BlockSpec((1, EXPERT_DIM, d), lambda i, e: (e, 0, 0)),
            pl.BlockSpec((1, d), lambda i, e: (0, 0)),
            pl.BlockSpec((1, d), lambda i, e: (0, 0)),
        ],
        out_specs=pl.BlockSpec((tm, d), lambda i, e: (i, 0)),
        out_shape=jax.ShapeDtypeStruct((t, d), F32),
        scratch_shapes=[pltpu.VMEM((tm, d), F32), pltpu.VMEM((tm, d), BF16)],
        compiler_params=_cparams(("parallel", "arbitrary")), name="moe_ln2",
    )(h, comb, p["w_expert_gate"], p["w_expert_up"], p["w_expert_down"], p["ln2_g"], p["ln2_b"])


_PAGES_PER_STEP = 8


def _page_specs(block, n):
    zeros = (0,) * (len(block) - 1)

    def spec(r):
        return pl.BlockSpec(block, lambda b, j, pt: (pt[b, j * n + r],) + zeros)
    return [spec(r) for r in range(n)]


def _sample_scores_kernel(pt_ref, qi_ref, wi_ref, *refs):
    page_refs, out_ref = refs[:-1], refs[-1]
    q_hi, q_lo = _split_bf16(qi_ref[0])
    q2 = jnp.concatenate([q_hi, q_lo], axis=0)
    wi = wi_ref[0][:, 0:1]
    pages = page_refs[0][0] if len(page_refs) == 1 else jnp.concatenate([p[0] for p in page_refs], axis=1)
    k_hi, k_lo = _split_bf16(pages)
    d2 = jnp.dot(q2, k_hi, preferred_element_type=F32)
    d = d2[0:8] + d2[8:16] + jnp.dot(q_hi, k_lo, preferred_element_type=F32)
    s = jnp.sum(jnp.maximum(d, 0.0) * wi, axis=0, keepdims=True)
    for r in range(len(page_refs)):
        out_ref[0, r:r + 1, :] = s[:, r * PAGE_SIZE:(r + 1) * PAGE_SIZE]


def _sample_scores(page_table, qi_s, wi_s, kidx_t, *, n):
    db, n_pages = page_table.shape
    return pl.pallas_call(
        _sample_scores_kernel,
        grid_spec=pltpu.PrefetchScalarGridSpec(
            num_scalar_prefetch=1, grid=(db, n_pages // n),
            in_specs=[pl.BlockSpec((1, 8, IDX_DIM), lambda b, j, pt: (b, 0, 0)),
                      pl.BlockSpec((1, 8, LANES), lambda b, j, pt: (b, 0, 0))]
            + _page_specs((1, IDX_DIM, PAGE_SIZE), n),
            out_specs=pl.BlockSpec((1, n, PAGE_SIZE), lambda b, j, pt: (b, j, 0)),
        ),
        out_shape=jax.ShapeDtypeStruct((db, n_pages, PAGE_SIZE), F32),
        compiler_params=_cparams(("parallel", "arbitrary")), name="sample_scores",
    )(page_table, qi_s, wi_s, *([kidx_t] * n))


def _sample_select_kernel(sc_ref, nvalid_ref, sel_ref, *, tk):
    nk, tq = sc_ref.shape
    nkc = nk // tk

    def chunk_reduce(fn, init):
        def body(c, acc):
            r0 = pl.multiple_of(c * tk, tk)
            return fn(acc, sc_ref[pl.ds(r0, tk), :])
        return lax.fori_loop(0, nkc, body, init)

    inf = jnp.float32(jnp.inf)
    smax = chunk_reduce(lambda a, blk: jnp.maximum(a, jnp.max(blk, axis=0, keepdims=True)),
                        jnp.full((1, tq), -inf, F32))
    smin = chunk_reduce(
        lambda a, blk: jnp.minimum(a, jnp.min(jnp.where(blk == -inf, inf, blk), axis=0, keepdims=True)),
        jnp.full((1, tq), inf, F32))

    def make_count(strict):
        def count(v):
            def fn(acc, blk):
                ind = jnp.where((blk > v) if strict else (blk >= v), 1.0, 0.0)
                return acc + jnp.sum(ind.reshape(tk // 8, 8, tq), axis=0)
            return jnp.sum(chunk_reduce(fn, jnp.zeros((8, tq), F32)), axis=0, keepdims=True)
        return count

    thr, m_take = _select_threshold(make_count(False), make_count(True), smin, smax, nvalid_ref[...])
    tri = (lax.broadcasted_iota(jnp.int32, (tk, tk), 0) >=
           lax.broadcasted_iota(jnp.int32, (tk, tk), 1)).astype(BF16)

    def emit(c, tie_seen):
        r0 = pl.multiple_of(c * tk, tk)
        blk = sc_ref[pl.ds(r0, tk), :]
        tie = jnp.where(blk == thr, 1.0, 0.0)
        rank = jnp.dot(tri, tie.astype(BF16), preferred_element_type=F32) + tie_seen
        keep_tie = tie * jnp.where(rank <= m_take, 1.0, 0.0)
        sel_ref[pl.ds(r0, tk), :] = jnp.where(blk > thr, 1.0, keep_tie)
        return tie_seen + jnp.sum(tie, axis=0, keepdims=True)

    lax.fori_loop(0, nkc, emit, jnp.zeros((1, tq), F32))


def _sample_select(sc_t, n_valid, *, tk):
    nk, tq = sc_t.shape
    return pl.pallas_call(
        functools.partial(_sample_select_kernel, tk=tk),
        grid=(1,),
        in_specs=[pl.BlockSpec((nk, tq), lambda i: (0, 0)), pl.BlockSpec((1, tq), lambda i: (0, 0))],
        out_specs=pl.BlockSpec((nk, tq), lambda i: (0, 0)),
        out_shape=jax.ShapeDtypeStruct((nk, tq), F32),
        compiler_params=_cparams(("arbitrary",)), name="sample_select",
    )(sc_t, n_valid)


def _sample_attend_kernel(pt_ref, qb_ref, sel_ref, knew_ref, vnew_ref, selnew_ref, *refs, n):
    k_refs, v_refs = refs[:n], refs[n:2 * n]
    out_ref, acc_ref, m_ref, l_ref = refs[2 * n:]
    j = pl.program_id(1)
    lane = lax.broadcasted_iota(jnp.int32, (1, PAGE_SIZE), 1)

    def update(h, lgs, vals):
        m_old = m_ref[h:h + 1, :]
        m_new = m_old
        for lg in lgs:
            m_new = jnp.maximum(m_new, jnp.max(lg, axis=1, keepdims=True))
        alpha = jnp.exp(m_old - m_new)
        l_new = alpha * l_ref[h:h + 1, :]
        acc = acc_ref[h] * alpha
        for lg, val in zip(lgs, vals):
            p = jnp.exp(lg - m_new)
            l_new = l_new + jnp.sum(p, axis=1, keepdims=True)
            acc = acc + val * p
        m_ref[h:h + 1, :] = m_new
        l_ref[h:h + 1, :] = l_new
        acc_ref[h] = acc

    @pl.when(j == 0)
    def _():
        m_ref[...] = jnp.full_like(m_ref, MASK_BIAS)
        l_ref[...] = jnp.zeros_like(l_ref)
        acc_ref[...] = jnp.zeros_like(acc_ref)
        for h in range(N_HEADS):
            lg = jnp.sum(knew_ref[0, h] * qb_ref[0, h], axis=0, keepdims=True)
            keep = jnp.logical_and(lane == 0, selnew_ref[0, h:h + 1, :] > 0.0)
            update(h, [jnp.where(keep, lg, MASK_BIAS)], [vnew_ref[0, h]])

    biases = [jnp.where(sel_ref[0, r:r + 1, :] > 0.0, 0.0, MASK_BIAS) for r in range(n)]
    for h in range(N_HEADS):
        q_h = qb_ref[0, h]
        lgs = [jnp.sum(k_refs[r][0, h] * q_h, axis=0, keepdims=True) + biases[r] for r in range(n)]
        update(h, lgs, [v_refs[r][0, h] for r in range(n)])

    @pl.when(j == pl.num_programs(1) - 1)
    def _():
        for h in range(N_HEADS):
            out_ref[0, h] = jnp.sum(acc_ref[h], axis=1, keepdims=True) / l_ref[h:h + 1, 0:1]


def _sample_attend(page_table, qb, sel, k_new, v_new, sel_new, ck_t, cv_t):
    db, n_pages = page_table.shape
    n = _PAGES_PER_STEP
    head_tile = (1, N_HEADS, HEAD_DIM, PAGE_SIZE)
    per_sample = pl.BlockSpec(head_tile, lambda b, j, pt: (b, 0, 0, 0))
    return pl.pallas_call(
        functools.partial(_sample_attend_kernel, n=n),
        grid_spec=pltpu.PrefetchScalarGridSpec(
            num_scalar_prefetch=1, grid=(db, n_pages // n),
            in_specs=[per_sample,
                      pl.BlockSpec((1, n, PAGE_SIZE), lambda b, j, pt: (b, j, 0)),
                      per_sample, per_sample,
                      pl.BlockSpec((1, N_HEADS, PAGE_SIZE), lambda b, j, pt: (b, 0, 0))]
            + _page_specs(head_tile, n) + _page_specs(head_tile, n),
            out_specs=pl.BlockSpec((1, N_HEADS, HEAD_DIM, 1), lambda b, j, pt: (b, 0, 0, 0)),
            scratch_shapes=[pltpu.VMEM((N_HEADS, HEAD_DIM, PAGE_SIZE), F32),
                            pltpu.VMEM((N_HEADS, PAGE_SIZE), F32), pltpu.VMEM((N_HEADS, PAGE_SIZE), F32)],
        ),
        out_shape=jax.ShapeDtypeStruct((db, N_HEADS, HEAD_DIM, 1), F32),
        compiler_params=_cparams(("parallel", "arbitrary")), name="sample_attend",
    )(page_table, qb, sel, k_new, v_new, sel_new, *([ck_t] * n), *([cv_t] * n))


def _rope_tables(pos):
    inv_freq = jnp.power(jnp.float32(ROPE_THETA), -jnp.arange(0, ROPE_ROT, 2, dtype=F32) / ROPE_ROT)
    ang = pos.astype(F32)[:, None] * inv_freq[None, :]
    return jnp.cos(ang).T, jnp.sin(ang).T


def _prepare_weights(p):
    w_in = p["w_in"]
    sizes = [ATTN_WIDTH, ATTN_WIDTH, ATTN_WIDTH, IDX_HEADS * IDX_DIM, IDX_DIM, IDX_HEADS,
             GMLP_WIDTH, GMLP_WIDTH, D_MODEL, D_MODEL]
    offs = np.concatenate([[0], np.cumsum(sizes)])
    n_attn = int(offs[6])
    wt_a_f32 = jnp.pad(w_in[:, :n_attn].T, ((0, _ROWS_A - n_attn), (0, 0)))
    wt_a = wt_a_f32.astype(BF16)
    wt_a_lo = (wt_a_f32 - wt_a.astype(F32)).astype(BF16)
    w_uv = w_in[:, int(offs[6]):int(offs[8])].astype(BF16)
    w_gates = w_in[:, int(offs[8]):].astype(BF16)
    w_re = jnp.transpose(p["w_router_expert"], (1, 0, 2)).reshape(D_MODEL, N_EXPERTS)
    w_router = jnp.pad(jnp.concatenate([w_re, p["w_router_group"]], axis=1),
                       ((0, 0), (0, LANES - N_EXPERTS - N_GROUPS)))
    b_router = jnp.pad(jnp.concatenate([p["b_router_expert"].reshape(-1), p["b_router_group"]]),
                       (0, LANES - N_EXPERTS - N_GROUPS)).reshape(1, LANES)
    return {
        "wt_a": wt_a, "wt_a_lo": wt_a_lo, "w_uv": w_uv, "w_gates": w_gates,
        "gate_bias": p["gate_bias"],
        "ln_v_g": p["ln_v_g"].reshape(1, -1), "ln_v_b": p["ln_v_b"].reshape(1, -1),
        "w_spatial": p["w_spatial"], "b_spatial_t": p["b_spatial"].T,
        "w_branch_a": p["w_branch_a"].astype(BF16), "w_branch_b": p["w_branch_b"].astype(BF16),
        "w_o": p["w_o"].astype(BF16),
        "ln1_g": p["ln1_g"].reshape(1, -1), "ln1_b": p["ln1_b"].reshape(1, -1),
        "ln2_g": p["ln2_g"].reshape(1, -1), "ln2_b": p["ln2_b"].reshape(1, -1),
        "w_router": w_router, "b_router": b_router,
        "w_expert_gate": p["w_expert_gate"].reshape(N_EXPERTS, D_MODEL, EXPERT_DIM).astype(BF16),
        "w_expert_up": p["w_expert_up"].reshape(N_EXPERTS, D_MODEL, EXPERT_DIM).astype(BF16),
        "w_expert_down": p["w_expert_down"].reshape(N_EXPERTS, EXPERT_DIM, D_MODEL).astype(BF16),
    }


def _pick_tile(n, pref):
    t = min(n, pref)
    assert n % t == 0, (n, pref)
    return t


def _channel_mix(x, attn, gm, p, *, precise_router):
    b, s, d = x.shape
    h, comb = _merge(x, attn, gm, p, tm=_pick_tile(s, 512), precise_router=precise_router)
    y = _moe(h.reshape(b * s, d), comb.reshape(b * s, LANES), p, tm=_pick_tile(b * s, 1024))
    return y.reshape(b, s, d)


def _prompt_step(x, p):
    b, s, _ = x.shape
    cos_t, sin_t = _rope_tables(jnp.arange(s, dtype=jnp.int32))
    tq = 256
    qz, kn, knb, vn, vtb, qit, kin, kinb, wit = _proj_attn(
        x, p["wt_a"], cos_t, sin_t, tm=512, tk=tq, q_scale=HEAD_DIM ** -0.5 * LOG2_E)
    attn = _prompt_attn(qz, knb, vtb, qit, kinb, wit, tq=tq)
    gm, _ = _gmlp(x, p["w_uv"], p["ln_v_g"], p["ln_v_b"], p["w_spatial"], p["b_spatial_t"],
                  tm=512, single_row_chunks=False)
    y = _channel_mix(x, attn, gm, p, precise_router=False)
    return y, _rows_from_t(kn), _rows_from_t(vn), jnp.transpose(kin, (0, 2, 1))


def _rows_from_t(a_t):
    b, _, s = a_t.shape
    return jnp.transpose(a_t.reshape(b, N_HEADS, HEAD_DIM, s), (0, 3, 1, 2))


def _sample_step(x, cache_k, cache_v, cache_kidx, page_table, p):
    db, t_new, d = x.shape
    assert t_new == 1
    n_pages = page_table.shape[1]
    past = n_pages * PAGE_SIZE
    rows = LANES
    assert db <= rows
    xs = jnp.pad(x.reshape(1, db, d), ((0, 0), (0, rows - db), (0, 0)))
    cos_t, sin_t = _rope_tables(jnp.full((rows,), past, jnp.int32))
    qz, kn, knb, vn, vtb, qit, kin, kinb, wit = _proj_attn(
        xs, p["wt_a"], cos_t, sin_t, tm=rows, tk=rows, q_scale=HEAD_DIM ** -0.5, wt_a_lo=p["wt_a_lo"])
    gm, chunk_v = _gmlp(xs, p["w_uv"], p["ln_v_g"], p["ln_v_b"], p["w_spatial"], p["b_spatial_t"],
                        tm=rows, single_row_chunks=True)

    qi_s = jnp.pad(qit[0].T.reshape(rows, IDX_HEADS, IDX_DIM), ((0, 0), (0, 8 - IDX_HEADS), (0, 0)))[:db]
    wi_s = jnp.broadcast_to(wit[0].T[:db, :, None], (db, 8, LANES))
    kidx_t = jnp.transpose(cache_kidx, (0, 2, 1))
    sc_past = _sample_scores(page_table, qi_s, wi_s, kidx_t, n=_PAGES_PER_STEP)
    own_page = jnp.pad(kin[0, :, :db].T[:, :, None], ((0, 0), (0, 0), (0, PAGE_SIZE - 1)))
    own_table = jnp.arange(db, dtype=jnp.int32).reshape(db, 1)
    sc_new = _sample_scores(own_table, qi_s, wi_s, own_page, n=1)[:, 0, 0]
    tk = 256
    nk = past + tk
    sc_all = jnp.concatenate([sc_past.reshape(db, past), sc_new[:, None],
                              jnp.full((db, tk - 1), -jnp.inf, F32)], axis=1)
    sc_t = jnp.pad(sc_all.T, ((0, 0), (0, rows - db)))
    n_valid = jnp.where(jnp.arange(rows) < db, float(past + 1), 0.0).astype(F32).reshape(1, rows)
    sel_t = _sample_select(sc_t, n_valid, tk=tk)
    sel = sel_t.T[:db]
    sel_past = sel[:, :past].reshape(db, n_pages, PAGE_SIZE)
    sel_new = jnp.broadcast_to(sel[:, past][:, None, None], (db, N_HEADS, PAGE_SIZE))

    def lane_replicated(a):
        return jnp.broadcast_to(a.astype(F32)[..., None], a.shape + (PAGE_SIZE,))

    q_s = jnp.stack([qz[0, hh, (hh % 2) * HEAD_DIM:(hh % 2 + 1) * HEAD_DIM, :db].T for hh in range(N_HEADS)],
                    axis=1)
    k_new = kn[0, :, :db].T.reshape(db, N_HEADS, HEAD_DIM)
    v_new = vn[0, :, :db].T.reshape(db, N_HEADS, HEAD_DIM)
    attn = _sample_attend(page_table, lane_replicated(q_s), sel_past, lane_replicated(k_new),
                          lane_replicated(v_new), sel_new,
                          jnp.transpose(cache_k, (0, 2, 3, 1)), jnp.transpose(cache_v, (0, 2, 3, 1)))
    attn_p = jnp.pad(attn.reshape(1, db, ATTN_WIDTH), ((0, 0), (0, rows - db), (0, 0))).astype(BF16)
    y = _channel_mix(xs, attn_p, gm, p, precise_router=True)
    return (y[0, :db].reshape(db, 1, d),
            k_new.reshape(db, 1, N_HEADS, HEAD_DIM), v_new.reshape(db, 1, N_HEADS, HEAD_DIM),
            kin[0, :, :db].T.reshape(db, 1, IDX_DIM), chunk_v[0, :db].reshape(db, 1, GMLP_WIDTH))


def kernel(x_prompt, x_sample, cache_k, cache_v, cache_kidx, page_table, w_in, gate_bias,
           ln_v_g, ln_v_b, w_spatial, b_spatial, w_branch_a, w_branch_b, w_o, ln1_g, ln1_b,
           w_router_group, b_router_group, w_router_expert, b_router_expert,
           w_expert_gate, w_expert_up, w_expert_down, ln2_g, ln2_b):
    p = _prepare_weights(dict(
        w_in=w_in, gate_bias=gate_bias, ln_v_g=ln_v_g, ln_v_b=ln_v_b, w_spatial=w_spatial,
        b_spatial=b_spatial, w_branch_a=w_branch_a, w_branch_b=w_branch_b, w_o=w_o, ln1_g=ln1_g,
        ln1_b=ln1_b, w_router_group=w_router_group, b_router_group=b_router_group,
        w_router_expert=w_router_expert, b_router_expert=b_router_expert,
        w_expert_gate=w_expert_gate, w_expert_up=w_expert_up, w_expert_down=w_expert_down,
        ln2_g=ln2_g, ln2_b=ln2_b))
    y_p, k_p, v_p, ki_p = _prompt_step(x_prompt, p)
    y_s, k_s, v_s, ki_s, cv_s = _sample_step(x_sample, cache_k, cache_v, cache_kidx, page_table, p)
    return (y_p, y_s, k_p, v_p, ki_p, k_s, v_s, ki_s, cv_s)
```

```python
import functools

import jax
import jax.numpy as jnp
import numpy as np
from jax import lax
from jax.experimental import pallas as pl
from jax.experimental.pallas import tpu as pltpu

F32 = jnp.float32
BF16 = jnp.bfloat16

D_MODEL = 1024
N_HEADS = 8
HEAD_DIM = 64
ATTN_WIDTH = N_HEADS * HEAD_DIM
IDX_HEADS = 4
IDX_DIM = 64
TOPK = 256
ROPE_THETA = 500000.0
ROPE_ROT = HEAD_DIM // 4
ROPE_HALF = ROPE_ROT // 2
CHUNK = 128
GMLP_GROUPS = 4
GMLP_WIDTH = D_MODEL // 2
N_GROUPS = 4
EXPERTS_PER_GROUP = 8
N_EXPERTS = N_GROUPS * EXPERTS_PER_GROUP
EXPERT_DIM = D_MODEL // 4
PAGE_SIZE = 128
LN_EPS = 1e-5
DEEPNORM_ALPHA = 2.0 ** 0.25
MASK_BIAS = -1e30

VMEM_LIMIT_V7X = 56 * 1024 * 1024
LANES = 128

_ROW_Q = 0
_ROW_K = _ROW_Q + ATTN_WIDTH
_ROW_V = _ROW_K + ATTN_WIDTH
_ROW_QI = _ROW_V + ATTN_WIDTH
_ROW_KI = _ROW_QI + IDX_HEADS * IDX_DIM
_ROW_WI = _ROW_KI + IDX_DIM
_WI_PAD = 16
_ROWS_A = _ROW_WI + _WI_PAD
_V_ROWS = HEAD_DIM + 16
LOG2_E = 1.4426950408889634


def _cparams(sem):
    return pltpu.CompilerParams(dimension_semantics=sem, vmem_limit_bytes=VMEM_LIMIT_V7X)


def _layer_norm(x, g, b):
    mu = jnp.mean(x, axis=-1, keepdims=True)
    xc = x - mu
    var = jnp.mean(xc * xc, axis=-1, keepdims=True)
    return xc * lax.rsqrt(var + LN_EPS) * g + b


def _rope_rows(seg, cos, sin):
    x1 = seg[0:ROPE_HALF]
    x2 = seg[ROPE_HALF:ROPE_ROT]
    return jnp.concatenate([x1 * cos - x2 * sin, x2 * cos + x1 * sin, seg[ROPE_ROT:]], axis=0)


def _split_bf16(a):
    hi = a.astype(BF16)
    return hi, (a - hi.astype(F32)).astype(BF16)


def _proj_attn_kernel(x_ref, wt_ref, *rest, tk, q_scale, precise_idx):
    if precise_idx:
        wtlo_ref, rest = rest[0], rest[1:]
    (cos_ref, sin_ref,
     qz_ref, kn_ref, knb_ref, vn_ref, vtb_ref, qit_ref, kin_ref, kinb_ref, wit_ref) = rest
    x = x_ref[0]
    xb = x.astype(BF16)
    tm = xb.shape[0]
    cos = cos_ref[...]
    sin = sin_ref[...]
    nt = (((1,), (1,)), ((), ()))

    def proj_t(r0, n):
        return lax.dot_general(wt_ref[r0:r0 + n, :], xb, nt, preferred_element_type=F32)

    if precise_idx:
        xlo = (x - xb.astype(F32)).astype(BF16)

        def proj_idx(r0, n):
            w_hi = wt_ref[r0:r0 + n, :]
            return (lax.dot_general(w_hi, xb, nt, preferred_element_type=F32)
                    + lax.dot_general(w_hi, xlo, nt, preferred_element_type=F32)
                    + lax.dot_general(wtlo_ref[r0:r0 + n, :], xb, nt, preferred_element_type=F32))
    else:
        proj_idx = proj_t

    q_t = proj_t(_ROW_Q, ATTN_WIDTH)
    zeros = jnp.zeros((HEAD_DIM, tm), F32)
    for h in range(N_HEADS):
        rot = _rope_rows(q_t[h * HEAD_DIM:(h + 1) * HEAD_DIM], cos, sin) * q_scale
        pair = [rot, zeros] if h % 2 == 0 else [zeros, rot]
        qz_ref[0, h] = jnp.concatenate(pair, axis=0).astype(BF16)

    k_t = proj_t(_ROW_K, ATTN_WIDTH)
    k_rot = jnp.concatenate(
        [_rope_rows(k_t[h * HEAD_DIM:(h + 1) * HEAD_DIM], cos, sin) for h in range(N_HEADS)], axis=0)
    kn_ref[0] = k_rot
    knb_ref[0] = k_rot.T.astype(BF16)

    v_t = proj_t(_ROW_V, ATTN_WIDTH)
    vn_ref[0] = v_t
    ones = jnp.ones((_V_ROWS - HEAD_DIM, tk), F32)
    for c in range(tm // tk):
        v_c = v_t[:, c * tk:(c + 1) * tk]
        vtb_ref[0, c] = jnp.concatenate(
            [piece for h in range(N_HEADS) for piece in (v_c[h * HEAD_DIM:(h + 1) * HEAD_DIM], ones)],
            axis=0).astype(BF16)

    qi_t = proj_idx(_ROW_QI, IDX_HEADS * IDX_DIM)
    qit_ref[0] = jnp.concatenate(
        [_rope_rows(qi_t[h * IDX_DIM:(h + 1) * IDX_DIM], cos, sin) for h in range(IDX_HEADS)],
        axis=0).astype(qit_ref.dtype)

    ki_t = _rope_rows(proj_idx(_ROW_KI, IDX_DIM), cos, sin)
    kin_ref[0] = ki_t
    kinb_ref[0] = ki_t.T.astype(BF16)

    wit_ref[0] = proj_idx(_ROW_WI, _WI_PAD)[0:8]


def _proj_attn(x, wt_a, cos_t, sin_t, *, tm, tk, q_scale, wt_a_lo=None):
    b, s, d = x.shape
    grid = (b, s // tm)
    precise_idx = wt_a_lo is not None
    out_shape = (
        jax.ShapeDtypeStruct((b, N_HEADS, 2 * HEAD_DIM, s), BF16),
        jax.ShapeDtypeStruct((b, ATTN_WIDTH, s), F32),
        jax.ShapeDtypeStruct((b, s, ATTN_WIDTH), BF16),
        jax.ShapeDtypeStruct((b, ATTN_WIDTH, s), F32),
        jax.ShapeDtypeStruct((b, s // tk, N_HEADS * _V_ROWS, tk), BF16),
        jax.ShapeDtypeStruct((b, IDX_HEADS * IDX_DIM, s), F32 if precise_idx else BF16),
        jax.ShapeDtypeStruct((b, IDX_DIM, s), F32),
        jax.ShapeDtypeStruct((b, s, IDX_DIM), BF16),
        jax.ShapeDtypeStruct((b, 8, s), F32),
    )
    out_specs = (
        pl.BlockSpec((1, N_HEADS, 2 * HEAD_DIM, tm), lambda bi, i: (bi, 0, 0, i)),
        pl.BlockSpec((1, ATTN_WIDTH, tm), lambda bi, i: (bi, 0, i)),
        pl.BlockSpec((1, tm, ATTN_WIDTH), lambda bi, i: (bi, i, 0)),
        pl.BlockSpec((1, ATTN_WIDTH, tm), lambda bi, i: (bi, 0, i)),
        pl.BlockSpec((1, tm // tk, N_HEADS * _V_ROWS, tk), lambda bi, i: (bi, i, 0, 0)),
        pl.BlockSpec((1, IDX_HEADS * IDX_DIM, tm), lambda bi, i: (bi, 0, i)),
        pl.BlockSpec((1, IDX_DIM, tm), lambda bi, i: (bi, 0, i)),
        pl.BlockSpec((1, tm, IDX_DIM), lambda bi, i: (bi, i, 0)),
        pl.BlockSpec((1, 8, tm), lambda bi, i: (bi, 0, i)),
    )
    weights = [wt_a] + ([wt_a_lo] if precise_idx else [])
    in_specs = (
        [pl.BlockSpec((1, tm, d), lambda bi, i: (bi, i, 0))]
        + [pl.BlockSpec((_ROWS_A, d), lambda bi, i: (0, 0)) for _ in weights]
        + [pl.BlockSpec((ROPE_HALF, tm), lambda bi, i: (0, i)),
           pl.BlockSpec((ROPE_HALF, tm), lambda bi, i: (0, i))])
    return pl.pallas_call(
        functools.partial(_proj_attn_kernel, tk=tk, q_scale=q_scale, precise_idx=precise_idx),
        grid=grid, in_specs=in_specs, out_specs=out_specs, out_shape=out_shape,
        compiler_params=_cparams(("parallel", "parallel")), name="proj_attn",
    )(x, *weights, cos_t, sin_t)


def _gmlp_kernel(x_ref, w_ref, g_ref, b_ref, ws_ref, bst_ref, gm_ref, vn_ref, *, single_row_chunks):
    xb = x_ref[0].astype(BF16)
    tm = xb.shape[0]
    u = jnp.dot(xb, w_ref[:, :GMLP_WIDTH], preferred_element_type=F32)
    vb = jnp.dot(xb, w_ref[:, GMLP_WIDTH:], preferred_element_type=F32)
    vn = _layer_norm(vb, g_ref[...], b_ref[...])
    vn_ref[0] = vn
    gw = GMLP_WIDTH // GMLP_GROUPS
    row = lax.broadcasted_iota(jnp.int32, (CHUNK, CHUNK), 0)
    col = lax.broadcasted_iota(jnp.int32, (CHUNK, CHUNK), 1)
    if single_row_chunks:
        for g in range(GMLP_GROUPS):
            w00 = ws_ref[g][0:1, 0:1].astype(BF16).astype(F32)
            vg = vn[:, g * gw:(g + 1) * gw].astype(BF16).astype(F32)
            mix = vg * w00 + bst_ref[0:1, g:g + 1]
            gm_ref[0, :, g * gw:(g + 1) * gw] = (u[:, g * gw:(g + 1) * gw] * mix).astype(BF16)
    else:
        vnb = vn.astype(BF16)
        for g in range(GMLP_GROUPS):
            wsg = jnp.where(row >= col, ws_ref[g], 0.0).astype(BF16)
            bias = bst_ref[:, g:g + 1]
            for c in range(tm // CHUNK):
                mix = jnp.dot(wsg, vnb[c * CHUNK:(c + 1) * CHUNK, g * gw:(g + 1) * gw],
                              preferred_element_type=F32) + bias
                gm_ref[0, c * CHUNK:(c + 1) * CHUNK, g * gw:(g + 1) * gw] = (
                    u[c * CHUNK:(c + 1) * CHUNK, g * gw:(g + 1) * gw] * mix).astype(BF16)


def _gmlp(x, w_uv, ln_g, ln_b, w_spatial, b_spatial_t, *, tm, single_row_chunks):
    b, s, d = x.shape
    return pl.pallas_call(
        functools.partial(_gmlp_kernel, single_row_chunks=single_row_chunks),
        grid=(b, s // tm),
        in_specs=[
            pl.BlockSpec((1, tm, d), lambda bi, i: (bi, i, 0)),
            pl.BlockSpec((d, 2 * GMLP_WIDTH), lambda bi, i: (0, 0)),
            pl.BlockSpec((1, GMLP_WIDTH), lambda bi, i: (0, 0)),
            pl.BlockSpec((1, GMLP_WIDTH), lambda bi, i: (0, 0)),
            pl.BlockSpec((GMLP_GROUPS, CHUNK, CHUNK), lambda bi, i: (0, 0, 0)),
            pl.BlockSpec((CHUNK, GMLP_GROUPS), lambda bi, i: (0, 0)),
        ],
        out_specs=(pl.BlockSpec((1, tm, GMLP_WIDTH), lambda bi, i: (bi, i, 0)),
                   pl.BlockSpec((1, tm, GMLP_WIDTH), lambda bi, i: (bi, i, 0))),
        out_shape=(jax.ShapeDtypeStruct((b, s, GMLP_WIDTH), BF16),
                   jax.ShapeDtypeStruct((b, s, GMLP_WIDTH), F32)),
        compiler_params=_cparams(("parallel", "parallel")), name="gmlp",
    )(x, w_uv, ln_g, ln_b, w_spatial, b_spatial_t)


def _select_threshold(count_ge, count_gt, max_below, smin, smax, n_valid):
    k = float(TOPK)
    one = jnp.ones_like(smin)
    zero = jnp.zeros_like(smin)
    flag = lambda cnd: jnp.where(cnd, one, zero)
    live = flag(n_valid > k)
    lo, c_lo = smin, n_valid
    hi, c_hi = smax + (jnp.abs(smax) * 1e-6 + 1e-30), zero
    c0 = count_ge(zero)
    c0p = count_gt(zero)
    ge0 = flag(c0 >= k)
    raise_lo = live * ge0 * flag(zero > lo)
    lower_hi = live * (1.0 - ge0) * flag(zero < hi)
    lo, c_lo = jnp.where(raise_lo > 0.0, zero, lo), jnp.where(raise_lo > 0.0, c0, c_lo)
    hi, c_hi = jnp.where(lower_hi > 0.0, zero, hi), jnp.where(lower_hi > 0.0, c0, c_hi)
    live = live * (1.0 - ge0 * flag(c0p < k))
    log_k = float(np.log(k))

    def cond(st):
        _, _, c_lo, _, c_hi, live = st
        return jnp.sum(live * flag(c_lo - c_hi > 2.0)) > 0.0

    def body(st):
        it, lo, c_lo, hi, c_hi, live = st
        half = 0.5 * lo + 0.5 * hi
        log_lo = jnp.log(c_lo)
        frac = (log_lo - log_k) / jnp.maximum(log_lo - jnp.log(jnp.maximum(c_hi, 0.5)), 1e-6)
        interp = lo + (hi - lo) * jnp.clip(frac, 0.02, 0.98)
        inside = jnp.logical_and(interp > lo, interp < hi)
        mid = jnp.where(jnp.logical_and(it % 2 == 0, inside), interp, half)
        upd = live * flag(jnp.logical_and(half > lo, half < hi))
        c = count_ge(mid)
        ge = flag(c >= k)
        up, down = upd * ge > 0.0, upd * (1.0 - ge) > 0.0
        lo, c_lo = jnp.where(up, mid, lo), jnp.where(up, c, c_lo)
        hi, c_hi = jnp.where(down, mid, hi), jnp.where(down, c, c_hi)
        live = upd * flag(c != k)
        return it + 1, lo, c_lo, hi, c_hi, live

    _, lo, _, hi, _, live = lax.while_loop(cond, body, (jnp.int32(0), lo, c_lo, hi, c_hi, live))
    thr = jnp.where(live > 0.0, max_below(hi), lo)
    m = k - count_gt(thr)
    return thr, m


def _prompt_attn_kernel(qz_ref, knb_ref, vtb_ref, qit_ref, kinb_ref, wit_ref, out_ref,
                        sc_ref, acc_ref, *, tq):
    tk = tq
    i = pl.program_id(1)
    nkc = i + 1
    wi = wit_ref[0]
    inf = jnp.float32(jnp.inf)

    def raw_scores(c):
        r0 = pl.multiple_of(c * tk, tk)
        kic = kinb_ref[0, pl.ds(r0, tk), :]
        ds = [jnp.dot(kic, qit_ref[0, h * IDX_DIM:(h + 1) * IDX_DIM, :], preferred_element_type=F32)
              for h in range(IDX_HEADS)]
        s = jnp.maximum(ds[0], 0.0) * wi[0:1, :]
        for h in range(1, IDX_HEADS):
            s = s + jnp.maximum(ds[h], 0.0) * wi[h:h + 1, :]
        return r0, s

    def score_chunk(c, carry):
        smin, smax = carry
        r0, s = raw_scores(c)
        sc_ref[pl.ds(r0, tk), :] = s
        return (jnp.minimum(smin, jnp.min(s, axis=0, keepdims=True)),
                jnp.maximum(smax, jnp.max(s, axis=0, keepdims=True)))

    smin, smax = lax.fori_loop(0, i, score_chunk,
                               (jnp.full((1, tq), inf, F32), jnp.full((1, tq), -inf, F32)))
    r0, s = raw_scores(i)
    vis = lax.broadcasted_iota(jnp.int32, (tk, tq), 0) <= lax.broadcasted_iota(jnp.int32, (tk, tq), 1)
    s_vis = jnp.where(vis, s, -inf)
    sc_ref[pl.ds(r0, tk), :] = s_vis
    smax = jnp.maximum(smax, jnp.max(s_vis, axis=0, keepdims=True))
    smin = jnp.minimum(smin, jnp.min(jnp.where(vis, s, inf), axis=0, keepdims=True))

    @pl.when(nkc % 2 == 1)
    def _():
        sc_ref[pl.ds(pl.multiple_of(nkc * tk, tk), tk), :] = jnp.full((tk, tq), -inf, F32)

    def make_count(strict):
        def count(v):
            def body(c, acc):
                for half in range(2):
                    r0 = pl.multiple_of((2 * c + half) * tk, tk)
                    blk = sc_ref[pl.ds(r0, tk), :]
                    hit = (blk > v) if strict else (blk >= v)
                    acc = acc + jnp.sum(jnp.where(hit, 1.0, 0.0).reshape(tk // 8, 8, tq), axis=0)
                return acc
            acc = lax.fori_loop(0, (nkc + 1) // 2, body, jnp.zeros((8, tq), F32))
            return jnp.sum(acc, axis=0, keepdims=True)
        return count

    def max_below(v):
        def body(c, acc):
            blk = sc_ref[pl.ds(pl.multiple_of(c * tk, tk), tk), :]
            return jnp.maximum(acc, jnp.max(jnp.where(blk < v, blk, -inf).reshape(tk // 8, 8, tq), axis=0))
        acc = lax.fori_loop(0, nkc, body, jnp.full((8, tq), -inf, F32))
        return jnp.max(acc, axis=0, keepdims=True)

    n_valid = (i * tq + lax.broadcasted_iota(jnp.int32, (1, tq), 1) + 1).astype(F32)
    thr, m_take = _select_threshold(make_count(False), make_count(True), max_below, smin, smax, n_valid)

    acc_ref[...] = jnp.zeros_like(acc_ref)
    tri = (lax.broadcasted_iota(jnp.int32, (tk, tk), 0) >=
           lax.broadcasted_iota(jnp.int32, (tk, tk), 1)).astype(BF16)

    def attend_chunk(c, carry):
        tie_seen, ms, ls = carry
        r0 = pl.multiple_of(c * tk, tk)
        blk = sc_ref[pl.ds(r0, tk), :]
        tie = jnp.where(blk == thr, 1.0, 0.0).astype(BF16)
        rank = jnp.dot(tri, tie, preferred_element_type=F32) + tie_seen
        tie_rank = jnp.where(blk == thr, rank, inf)
        bias = jnp.where(blk > thr, 0.0, jnp.where(tie_rank <= m_take, 0.0, MASK_BIAS))
        kc = knb_ref[0, pl.ds(r0, tk), :]
        lgs = []
        for h in range(N_HEADS):
            k2 = kc[:, (h // 2) * 2 * HEAD_DIM:(h // 2 + 1) * 2 * HEAD_DIM]
            lgs.append(jnp.dot(k2, qz_ref[0, h], preferred_element_type=F32) + bias)
        new_ms = [jnp.maximum(ms[h], jnp.max(lgs[h], axis=0, keepdims=True)) for h in range(N_HEADS)]
        ps = [jnp.exp2(lgs[h] - new_ms[h]).astype(BF16) for h in range(N_HEADS)]
        pvs = [jnp.dot(vtb_ref[0, c, h * _V_ROWS:(h + 1) * _V_ROWS, :], ps[h], preferred_element_type=F32)
               for h in range(N_HEADS)]
        new_ls = []
        for h in range(N_HEADS):
            alpha = jnp.exp2(ms[h] - new_ms[h])
            acc_ref[h * HEAD_DIM:(h + 1) * HEAD_DIM, :] = (
                acc_ref[h * HEAD_DIM:(h + 1) * HEAD_DIM, :] * alpha + pvs[h][0:HEAD_DIM])
            new_ls.append(ls[h] * alpha + pvs[h][HEAD_DIM:HEAD_DIM + 1])
        return rank[tk - 1:tk, :], tuple(new_ms), tuple(new_ls)

    init = (jnp.zeros((1, tq), F32),
            tuple(jnp.full((1, tq), MASK_BIAS, F32) for _ in range(N_HEADS)),
            tuple(jnp.zeros((1, tq), F32) for _ in range(N_HEADS)))
    _, _, ls = lax.fori_loop(0, nkc, attend_chunk, init)

    out_t = jnp.concatenate(
        [acc_ref[h * HEAD_DIM:(h + 1) * HEAD_DIM, :] / ls[h] for h in range(N_HEADS)], axis=0)
    out_ref[0] = out_t.T.astype(BF16)


def _prompt_attn(qz, knb, vtb, qit, kinb, wit, *, tq):
    b, s, _ = knb.shape
    return pl.pallas_call(
        functools.partial(_prompt_attn_kernel, tq=tq),
        grid=(b, s // tq),
        in_specs=[
            pl.BlockSpec((1, N_HEADS, 2 * HEAD_DIM, tq), lambda bi, i: (bi, 0, 0, i)),
            pl.BlockSpec((1, s, ATTN_WIDTH), lambda bi, i: (bi, 0, 0), pipeline_mode=pl.Buffered(1)),
            pl.BlockSpec((1, s // tq, N_HEADS * _V_ROWS, tq), lambda bi, i: (bi, 0, 0, 0),
                         pipeline_mode=pl.Buffered(1)),
            pl.BlockSpec((1, IDX_HEADS * IDX_DIM, tq), lambda bi, i: (bi, 0, i)),
            pl.BlockSpec((1, s, IDX_DIM), lambda bi, i: (bi, 0, 0), pipeline_mode=pl.Buffered(1)),
            pl.BlockSpec((1, 8, tq), lambda bi, i: (bi, 0, i)),
        ],
        out_specs=pl.BlockSpec((1, tq, ATTN_WIDTH), lambda bi, i: (bi, i, 0)),
        out_shape=jax.ShapeDtypeStruct((b, s, ATTN_WIDTH), BF16),
        scratch_shapes=[
            pltpu.VMEM((s + tq, tq), F32),
            pltpu.VMEM((ATTN_WIDTH, tq), F32),
        ],
        compiler_params=_cparams(("parallel", "arbitrary")), name="prompt_attn",
    )(qz, knb, vtb, qit, kinb, wit)


def _route(rl):
    lane = lax.broadcasted_iota(jnp.int32, rl.shape, 1).astype(F32)
    neg = jnp.float32(-jnp.inf)
    big = jnp.float32(1e9)
    g0 = float(N_EXPERTS)
    lg = jnp.where(jnp.logical_and(lane >= g0, lane < g0 + N_GROUPS), rl, neg)
    gmax = jnp.max(lg, axis=1, keepdims=True)
    gsel = jnp.min(jnp.where(lg == gmax, lane, big), axis=1, keepdims=True) - g0
    pgsel = 1.0 / jnp.sum(jnp.exp(lg - gmax), axis=1, keepdims=True)
    e0 = gsel * EXPERTS_PER_GROUP
    le = jnp.where(jnp.logical_and(lane >= e0, lane < e0 + EXPERTS_PER_GROUP), rl, neg)
    v1 = jnp.max(le, axis=1, keepdims=True)
    i1 = jnp.min(jnp.where(le == v1, lane, big), axis=1, keepdims=True)
    le2 = jnp.where(lane == i1, neg, le)
    v2 = jnp.max(le2, axis=1, keepdims=True)
    i2 = jnp.min(jnp.where(le2 == v2, lane, big), axis=1, keepdims=True)
    ex = jnp.exp(v2 - v1)
    den = 1.0 + ex
    p1 = pgsel / den
    p2 = pgsel * ex / den
    return jnp.where(lane == i1, p1, jnp.where(lane == i2, p2, 0.0))


def _merge_kernel(x_ref, attn_ref, gm_ref, wgate_ref, gbias_ref, wa_ref, wb_ref, wo_ref,
                  lng_ref, lnb_ref, wr_ref, br_ref, h_ref, comb_ref, *, precise_router):
    x = x_ref[0]
    xb = x.astype(BF16)
    ga = jax.nn.sigmoid(jnp.dot(xb, wgate_ref[:, :D_MODEL], preferred_element_type=F32) + gbias_ref[0:1, :])
    gb = jax.nn.sigmoid(jnp.dot(xb, wgate_ref[:, D_MODEL:], preferred_element_type=F32) + gbias_ref[1:2, :])
    a = jnp.dot(attn_ref[0], wa_ref[...], preferred_element_type=F32)
    bm = jnp.dot(gm_ref[0], wb_ref[...], preferred_element_type=F32)
    merged = ga * a + gb * bm
    y1 = DEEPNORM_ALPHA * x + jnp.dot(merged.astype(BF16), wo_ref[...], preferred_element_type=F32)
    h = _layer_norm(y1, lng_ref[...], lnb_ref[...])
    h_ref[0] = h
    if precise_router:
        h_hi, h_lo = _split_bf16(h)
        w_hi, w_lo = _split_bf16(wr_ref[...])
        rl = (jnp.dot(h_hi, w_hi, preferred_element_type=F32) + jnp.dot(h_lo, w_hi, preferred_element_type=F32)
              + jnp.dot(h_hi, w_lo, preferred_element_type=F32))
    else:
        rl = jnp.dot(h.astype(BF16), wr_ref[...].astype(BF16), preferred_element_type=F32)
    comb_ref[0] = _route(rl + br_ref[...])


def _merge(x, attn, gm, p, *, tm, precise_router):
    b, s, d = x.shape
    const = lambda shape: pl.BlockSpec(shape, lambda bi, i: tuple(0 for _ in shape))
    return pl.pallas_call(
        functools.partial(_merge_kernel, precise_router=precise_router),
        grid=(b, s // tm),
        in_specs=[
            pl.BlockSpec((1, tm, d), lambda bi, i: (bi, i, 0)),
            pl.BlockSpec((1, tm, ATTN_WIDTH), lambda bi, i: (bi, i, 0)),
            pl.BlockSpec((1, tm, GMLP_WIDTH), lambda bi, i: (bi, i, 0)),
            const((d, 2 * d)), const((2, d)), const((ATTN_WIDTH, d)), const((GMLP_WIDTH, d)), const((d, d)),
            const((1, d)), const((1, d)), const((d, LANES)), const((1, LANES)),
        ],
        out_specs=(pl.BlockSpec((1, tm, d), lambda bi, i: (bi, i, 0)),
                   pl.BlockSpec((1, tm, LANES), lambda bi, i: (bi, i, 0))),
        out_shape=(jax.ShapeDtypeStruct((b, s, d), F32), jax.ShapeDtypeStruct((b, s, LANES), F32)),
        compiler_params=_cparams(("parallel", "parallel")), name="merge_ln1_route",
    )(x, attn, gm, p["w_gates"], p["gate_bias"], p["w_branch_a"], p["w_branch_b"], p["w_o"],
      p["ln1_g"], p["ln1_b"], p["w_router"], p["b_router"])


def _moe_kernel(h_ref, comb_ref, wg_ref, wu_ref, wd_ref, lng_ref, lnb_ref, y_ref, acc_ref, hb_ref):
    e = pl.program_id(1)

    @pl.when(e == 0)
    def _():
        acc_ref[...] = jnp.zeros_like(acc_ref)
        hb_ref[...] = h_ref[...].astype(BF16)

    comb = comb_ref[...]
    lane = lax.broadcasted_iota(jnp.int32, comb.shape, 1)
    wcol = jnp.sum(jnp.where(lane == e, comb, 0.0), axis=1, keepdims=True)
    tm = hb_ref.shape[0]
    halves = [slice(r, r + tm // 2) for r in (0, tm // 2)] if tm % 16 == 0 else [slice(0, tm)]
    gates = [jnp.dot(hb_ref[rows, :], wg_ref[0], preferred_element_type=F32) for rows in halves]
    ups = [jnp.dot(hb_ref[rows, :], wu_ref[0], preferred_element_type=F32) for rows in halves]
    for rows, gate, up in zip(halves, gates, ups):
        hg = jax.nn.silu(gate) * up * wcol[rows]
        acc_ref[rows, :] += jnp.dot(hg.astype(BF16), wd_ref[0], preferred_element_type=F32)

    @pl.when(e == pl.num_programs(1) - 1)
    def _():
        y_ref[...] = _layer_norm(DEEPNORM_ALPHA * h_ref[...] + acc_ref[...], lng_ref[...], lnb_ref[...])


def _moe(h, comb, p, *, tm):
    t, d = h.shape
    return pl.pallas_call(
        _moe_kernel,
        grid=(t // tm, N_EXPERTS),
        in_specs=[
            pl.BlockSpec((tm, d), lambda i, e: (i, 0)),
            pl.BlockSpec((tm, LANES), lambda i, e: (i, 0)),
            pl.BlockSpec((1, d, EXPERT_DIM), lambda i, e: (e, 0, 0)),
            pl.BlockSpec((1, d, EXPERT_DIM), lambda i, e: (e, 0, 0)),
            pl.BlockSpec((1, EXPERT_DIM, d), lambda i, e: (e, 0, 0)),
            pl.BlockSpec((1, d), lambda i, e: (0, 0)),
            pl.BlockSpec((1, d), lambda i, e: (0, 0)),
        ],
        out_specs=pl.BlockSpec((tm, d), lambda i, e: (i, 0)),
        out_shape=jax.ShapeDtypeStruct((t, d), F32),
        scratch_shapes=[pltpu.VMEM((tm, d), F32), pltpu.VMEM((tm, d), BF16)],
        compiler_params=_cparams(("parallel", "arbitrary")), name="moe_ln2",
    )(h, comb, p["w_expert_gate"], p["w_expert_up"], p["w_expert_down"], p["ln2_g"], p["ln2_b"])


_PAGES_PER_STEP = 8


def _page_specs(block, n):
    zeros = (0,) * (len(block) - 1)

    def spec(r):
        return pl.BlockSpec(block, lambda b, j, pt: (pt[b, j * n + r],) + zeros)
    return [spec(r) for r in range(n)]


def _sample_scores_kernel(pt_ref, qi_ref, wi_ref, *refs):
    page_refs, out_ref = refs[:-1], refs[-1]
    q_hi, q_lo = _split_bf16(qi_ref[0])
    q2 = jnp.concatenate([q_hi, q_lo], axis=0)
    wi = wi_ref[0][:, 0:1]
    pages = page_refs[0][0] if len(page_refs) == 1 else jnp.concatenate([p[0] for p in page_refs], axis=1)
    k_hi, k_lo = _split_bf16(pages)
    d2 = jnp.dot(q2, k_hi, preferred_element_type=F32)
    d = d2[0:8] + d2[8:16] + jnp.dot(q_hi, k_lo, preferred_element_type=F32)
    s = jnp.sum(jnp.maximum(d, 0.0) * wi, axis=0, keepdims=True)
    for r in range(len(page_refs)):
        out_ref[0, r:r + 1, :] = s[:, r * PAGE_SIZE:(r + 1) * PAGE_SIZE]


def _sample_scores(page_table, qi_s, wi_s, kidx_t, *, n):
    db, n_pages = page_table.shape
    return pl.pallas_call(
        _sample_scores_kernel,
        grid_spec=pltpu.PrefetchScalarGridSpec(
            num_scalar_prefetch=1, grid=(db, n_pages // n),
            in_specs=[pl.BlockSpec((1, 8, IDX_DIM), lambda b, j, pt: (b, 0, 0)),
                      pl.BlockSpec((1, 8, LANES), lambda b, j, pt: (b, 0, 0))]
            + _page_specs((1, IDX_DIM, PAGE_SIZE), n),
            out_specs=pl.BlockSpec((1, n, PAGE_SIZE), lambda b, j, pt: (b, j, 0)),
        ),
        out_shape=jax.ShapeDtypeStruct((db, n_pages, PAGE_SIZE), F32),
        compiler_params=_cparams(("parallel", "arbitrary")), name="sample_scores",
    )(page_table, qi_s, wi_s, *([kidx_t] * n))


def _sample_select_kernel(sc_ref, nvalid_ref, sel_ref, *, tk):
    nk, tq = sc_ref.shape
    nkc = nk // tk

    def chunk_reduce(fn, init):
        def body(c, acc):
            r0 = pl.multiple_of(c * tk, tk)
            return fn(acc, sc_ref[pl.ds(r0, tk), :])
        return lax.fori_loop(0, nkc, body, init)

    inf = jnp.float32(jnp.inf)
    smax = chunk_reduce(lambda a, blk: jnp.maximum(a, jnp.max(blk, axis=0, keepdims=True)),
                        jnp.full((1, tq), -inf, F32))
    smin = chunk_reduce(
        lambda a, blk: jnp.minimum(a, jnp.min(jnp.where(blk == -inf, inf, blk), axis=0, keepdims=True)),
        jnp.full((1, tq), inf, F32))

    def make_count(strict):
        def count(v):
            def fn(acc, blk):
                ind = jnp.where((blk > v) if strict else (blk >= v), 1.0, 0.0)
                return acc + jnp.sum(ind.reshape(tk // 8, 8, tq), axis=0)
            return jnp.sum(chunk_reduce(fn, jnp.zeros((8, tq), F32)), axis=0, keepdims=True)
        return count

    def max_below(v):
        fn = lambda acc, blk: jnp.maximum(
            acc, jnp.max(jnp.where(blk < v, blk, -inf).reshape(tk // 8, 8, tq), axis=0))
        return jnp.max(chunk_reduce(fn, jnp.full((8, tq), -inf, F32)), axis=0, keepdims=True)

    thr, m_take = _select_threshold(make_count(False), make_count(True), max_below, smin, smax,
                                    nvalid_ref[...])
    tri = (lax.broadcasted_iota(jnp.int32, (tk, tk), 0) >=
           lax.broadcasted_iota(jnp.int32, (tk, tk), 1)).astype(BF16)

    def emit(c, tie_seen):
        r0 = pl.multiple_of(c * tk, tk)
        blk = sc_ref[pl.ds(r0, tk), :]
        tie = jnp.where(blk == thr, 1.0, 0.0)
        rank = jnp.dot(tri, tie.astype(BF16), preferred_element_type=F32) + tie_seen
        keep_tie = tie * jnp.where(rank <= m_take, 1.0, 0.0)
        sel_ref[pl.ds(r0, tk), :] = jnp.where(blk > thr, 1.0, keep_tie)
        return tie_seen + jnp.sum(tie, axis=0, keepdims=True)

    lax.fori_loop(0, nkc, emit, jnp.zeros((1, tq), F32))


def _sample_select(sc_t, n_valid, *, tk):
    nk, tq = sc_t.shape
    return pl.pallas_call(
        functools.partial(_sample_select_kernel, tk=tk),
        grid=(1,),
        in_specs=[pl.BlockSpec((nk, tq), lambda i: (0, 0)), pl.BlockSpec((1, tq), lambda i: (0, 0))],
        out_specs=pl.BlockSpec((nk, tq), lambda i: (0, 0)),
        out_shape=jax.ShapeDtypeStruct((nk, tq), F32),
        compiler_params=_cparams(("arbitrary",)), name="sample_select",
    )(sc_t, n_valid)


def _sample_attend_kernel(pt_ref, qb_ref, sel_ref, knew_ref, vnew_ref, selnew_ref, *refs, n):
    k_refs, v_refs = refs[:n], refs[n:2 * n]
    out_ref, acc_ref, m_ref, l_ref = refs[2 * n:]
    j = pl.program_id(1)
    lane = lax.broadcasted_iota(jnp.int32, (1, PAGE_SIZE), 1)

    def update(h, lgs, vals):
        m_old = m_ref[h:h + 1, :]
        m_new = m_old
        for lg in lgs:
            m_new = jnp.maximum(m_new, jnp.max(lg, axis=1, keepdims=True))
        alpha = jnp.exp(m_old - m_new)
        l_new = alpha * l_ref[h:h + 1, :]
        acc = acc_ref[h] * alpha
        for lg, val in zip(lgs, vals):
            p = jnp.exp(lg - m_new)
            l_new = l_new + jnp.sum(p, axis=1, keepdims=True)
            acc = acc + val * p
        m_ref[h:h + 1, :] = m_new
        l_ref[h:h + 1, :] = l_new
        acc_ref[h] = acc

    @pl.when(j == 0)
    def _():
        m_ref[...] = jnp.full_like(m_ref, MASK_BIAS)
        l_ref[...] = jnp.zeros_like(l_ref)
        acc_ref[...] = jnp.zeros_like(acc_ref)
        for h in range(N_HEADS):
            lg = jnp.sum(knew_ref[0, h] * qb_ref[0, h], axis=0, keepdims=True)
            keep = jnp.logical_and(lane == 0, selnew_ref[0, h:h + 1, :] > 0.0)
            update(h, [jnp.where(keep, lg, MASK_BIAS)], [vnew_ref[0, h]])

    biases = [jnp.where(sel_ref[0, r:r + 1, :] > 0.0, 0.0, MASK_BIAS) for r in range(n)]
    for h in range(N_HEADS):
        q_h = qb_ref[0, h]
        lgs = [jnp.sum(k_refs[r][0, h] * q_h, axis=0, keepdims=True) + biases[r] for r in range(n)]
        update(h, lgs, [v_refs[r][0, h] for r in range(n)])

    @pl.when(j == pl.num_programs(1) - 1)
    def _():
        for h in range(N_HEADS):
            out_ref[0, h] = jnp.sum(acc_ref[h], axis=1, keepdims=True) / l_ref[h:h + 1, 0:1]


def _sample_attend(page_table, qb, sel, k_new, v_new, sel_new, ck_t, cv_t):
    db, n_pages = page_table.shape
    n = _PAGES_PER_STEP
    head_tile = (1, N_HEADS, HEAD_DIM, PAGE_SIZE)
    per_sample = pl.BlockSpec(head_tile, lambda b, j, pt: (b, 0, 0, 0))
    return pl.pallas_call(
        functools.partial(_sample_attend_kernel, n=n),
        grid_spec=pltpu.PrefetchScalarGridSpec(
            num_scalar_prefetch=1, grid=(db, n_pages // n),
            in_specs=[per_sample,
                      pl.BlockSpec((1, n, PAGE_SIZE), lambda b, j, pt: (b, j, 0)),
                      per_sample, per_sample,
                      pl.BlockSpec((1, N_HEADS, PAGE_SIZE), lambda b, j, pt: (b, 0, 0))]
            + _page_specs(head_tile, n) + _page_specs(head_tile, n),
            out_specs=pl.BlockSpec((1, N_HEADS, HEAD_DIM, 1), lambda b, j, pt: (b, 0, 0, 0)),
            scratch_shapes=[pltpu.VMEM((N_HEADS, HEAD_DIM, PAGE_SIZE), F32),
                            pltpu.VMEM((N_HEADS, PAGE_SIZE), F32), pltpu.VMEM((N_HEADS, PAGE_SIZE), F32)],
        ),
        out_shape=jax.ShapeDtypeStruct((db, N_HEADS, HEAD_DIM, 1), F32),
        compiler_params=_cparams(("parallel", "arbitrary")), name="sample_attend",
    )(page_table, qb, sel, k_new, v_new, sel_new, *([ck_t] * n), *([cv_t] * n))


def _rope_tables(pos):
    inv_freq = jnp.power(jnp.float32(ROPE_THETA), -jnp.arange(0, ROPE_ROT, 2, dtype=F32) / ROPE_ROT)
    ang = pos.astype(F32)[:, None] * inv_freq[None, :]
    return jnp.cos(ang).T, jnp.sin(ang).T


def _prepare_weights(p):
    w_in = p["w_in"]
    sizes = [ATTN_WIDTH, ATTN_WIDTH, ATTN_WIDTH, IDX_HEADS * IDX_DIM, IDX_DIM, IDX_HEADS,
             GMLP_WIDTH, GMLP_WIDTH, D_MODEL, D_MODEL]
    offs = np.concatenate([[0], np.cumsum(sizes)])
    n_attn = int(offs[6])
    wt_a_f32 = jnp.pad(w_in[:, :n_attn].T, ((0, _ROWS_A - n_attn), (0, 0)))
    wt_a = wt_a_f32.astype(BF16)
    wt_a_lo = (wt_a_f32 - wt_a.astype(F32)).astype(BF16)
    w_uv = w_in[:, int(offs[6]):int(offs[8])].astype(BF16)
    w_gates = w_in[:, int(offs[8]):].astype(BF16)
    w_re = jnp.transpose(p["w_router_expert"], (1, 0, 2)).reshape(D_MODEL, N_EXPERTS)
    w_router = jnp.pad(jnp.concatenate([w_re, p["w_router_group"]], axis=1),
                       ((0, 0), (0, LANES - N_EXPERTS - N_GROUPS)))
    b_router = jnp.pad(jnp.concatenate([p["b_router_expert"].reshape(-1), p["b_router_group"]]),
                       (0, LANES - N_EXPERTS - N_GROUPS)).reshape(1, LANES)
    return {
        "wt_a": wt_a, "wt_a_lo": wt_a_lo, "w_uv": w_uv, "w_gates": w_gates,
        "gate_bias": p["gate_bias"],
        "ln_v_g": p["ln_v_g"].reshape(1, -1), "ln_v_b": p["ln_v_b"].reshape(1, -1),
        "w_spatial": p["w_spatial"], "b_spatial_t": p["b_spatial"].T,
        "w_branch_a": p["w_branch_a"].astype(BF16), "w_branch_b": p["w_branch_b"].astype(BF16),
        "w_o": p["w_o"].astype(BF16),
        "ln1_g": p["ln1_g"].reshape(1, -1), "ln1_b": p["ln1_b"].reshape(1, -1),
        "ln2_g": p["ln2_g"].reshape(1, -1), "ln2_b": p["ln2_b"].reshape(1, -1),
        "w_router": w_router, "b_router": b_router,
        "w_expert_gate": p["w_expert_gate"].reshape(N_EXPERTS, D_MODEL, EXPERT_DIM).astype(BF16),
        "w_expert_up": p["w_expert_up"].reshape(N_EXPERTS, D_MODEL, EXPERT_DIM).astype(BF16),
        "w_expert_down": p["w_expert_down"].reshape(N_EXPERTS, EXPERT_DIM, D_MODEL).astype(BF16),
    }


def _pick_tile(n, pref):
    t = min(n, pref)
    assert n % t == 0, (n, pref)
    return t


def _channel_mix(x, attn, gm, p, *, precise_router):
    b, s, d = x.shape
    h, comb = _merge(x, attn, gm, p, tm=_pick_tile(s, 512), precise_router=precise_router)
    y = _moe(h.reshape(b * s, d), comb.reshape(b * s, LANES), p, tm=_pick_tile(b * s, 1024))
    return y.reshape(b, s, d)


def _prompt_step(x, p):
    b, s, _ = x.shape
    cos_t, sin_t = _rope_tables(jnp.arange(s, dtype=jnp.int32))
    tq = 256
    qz, kn, knb, vn, vtb, qit, kin, kinb, wit = _proj_attn(
        x, p["wt_a"], cos_t, sin_t, tm=512, tk=tq, q_scale=HEAD_DIM ** -0.5 * LOG2_E)
    attn = _prompt_attn(qz, knb, vtb, qit, kinb, wit, tq=tq)
    gm, _ = _gmlp(x, p["w_uv"], p["ln_v_g"], p["ln_v_b"], p["w_spatial"], p["b_spatial_t"],
                  tm=512, single_row_chunks=False)
    y = _channel_mix(x, attn, gm, p, precise_router=False)
    return y, _rows_from_t(kn), _rows_from_t(vn), jnp.transpose(kin, (0, 2, 1))


def _rows_from_t(a_t):
    b, _, s = a_t.shape
    return jnp.transpose(a_t.reshape(b, N_HEADS, HEAD_DIM, s), (0, 3, 1, 2))


def _sample_step(x, cache_k, cache_v, cache_kidx, page_table, p):
    db, t_new, d = x.shape
    assert t_new == 1
    n_pages = page_table.shape[1]
    past = n_pages * PAGE_SIZE
    rows = LANES
    assert db <= rows
    xs = jnp.pad(x.reshape(1, db, d), ((0, 0), (0, rows - db), (0, 0)))
    cos_t, sin_t = _rope_tables(jnp.full((rows,), past, jnp.int32))
    qz, kn, knb, vn, vtb, qit, kin, kinb, wit = _proj_attn(
        xs, p["wt_a"], cos_t, sin_t, tm=rows, tk=rows, q_scale=HEAD_DIM ** -0.5, wt_a_lo=p["wt_a_lo"])
    gm, chunk_v = _gmlp(xs, p["w_uv"], p["ln_v_g"], p["ln_v_b"], p["w_spatial"], p["b_spatial_t"],
                        tm=rows, single_row_chunks=True)

    qi_s = jnp.pad(qit[0].T.reshape(rows, IDX_HEADS, IDX_DIM), ((0, 0), (0, 8 - IDX_HEADS), (0, 0)))[:db]
    wi_s = jnp.broadcast_to(wit[0].T[:db, :, None], (db, 8, LANES))
    kidx_t = jnp.transpose(cache_kidx, (0, 2, 1))
    sc_past = _sample_scores(page_table, qi_s, wi_s, kidx_t, n=_PAGES_PER_STEP)
    own_page = jnp.pad(kin[0, :, :db].T[:, :, None], ((0, 0), (0, 0), (0, PAGE_SIZE - 1)))
    own_table = jnp.arange(db, dtype=jnp.int32).reshape(db, 1)
    sc_new = _sample_scores(own_table, qi_s, wi_s, own_page, n=1)[:, 0, 0]
    tk = 256
    nk = past + tk
    sc_all = jnp.concatenate([sc_past.reshape(db, past), sc_new[:, None],
                              jnp.full((db, tk - 1), -jnp.inf, F32)], axis=1)
    sc_t = jnp.pad(sc_all.T, ((0, 0), (0, rows - db)))
    n_valid = jnp.where(jnp.arange(rows) < db, float(past + 1), 0.0).astype(F32).reshape(1, rows)
    sel_t = _sample_select(sc_t, n_valid, tk=tk)
    sel = sel_t.T[:db]
    sel_past = sel[:, :past].reshape(db, n_pages, PAGE_SIZE)
    sel_new = jnp.broadcast_to(sel[:, past][:, None, None], (db, N_HEADS, PAGE_SIZE))

    def lane_replicated(a):
        return jnp.broadcast_to(a.astype(F32)[..., None], a.shape + (PAGE_SIZE,))

    q_s = jnp.stack([qz[0, hh, (hh % 2) * HEAD_DIM:(hh % 2 + 1) * HEAD_DIM, :db].T for hh in range(N_HEADS)],
                    axis=1)
    k_new = kn[0, :, :db].T.reshape(db, N_HEADS, HEAD_DIM)
    v_new = vn[0, :, :db].T.reshape(db, N_HEADS, HEAD_DIM)
    attn = _sample_attend(page_table, lane_replicated(q_s), sel_past, lane_replicated(k_new),
                          lane_replicated(v_new), sel_new,
                          jnp.transpose(cache_k, (0, 2, 3, 1)), jnp.transpose(cache_v, (0, 2, 3, 1)))
    attn_p = jnp.pad(attn.reshape(1, db, ATTN_WIDTH), ((0, 0), (0, rows - db), (0, 0))).astype(BF16)
    y = _channel_mix(xs, attn_p, gm, p, precise_router=True)
    return (y[0, :db].reshape(db, 1, d),
            k_new.reshape(db, 1, N_HEADS, HEAD_DIM), v_new.reshape(db, 1, N_HEADS, HEAD_DIM),
            kin[0, :, :db].T.reshape(db, 1, IDX_DIM), chunk_v[0, :db].reshape(db, 1, GMLP_WIDTH))


def kernel(x_prompt, x_sample, cache_k, cache_v, cache_kidx, page_table, w_in, gate_bias,
           ln_v_g, ln_v_b, w_spatial, b_spatial, w_branch_a, w_branch_b, w_o, ln1_g, ln1_b,
           w_router_group, b_router_group, w_router_expert, b_router_expert,
           w_expert_gate, w_expert_up, w_expert_down, ln2_g, ln2_b):
    p = _prepare_weights(dict(
        w_in=w_in, gate_bias=gate_bias, ln_v_g=ln_v_g, ln_v_b=ln_v_b, w_spatial=w_spatial,
        b_spatial=b_spatial, w_branch_a=w_branch_a, w_branch_b=w_branch_b, w_o=w_o, ln1_g=ln1_g,
        ln1_b=ln1_b, w_router_group=w_router_group, b_router_group=b_router_group,
        w_router_expert=w_router_expert, b_router_expert=b_router_expert,
        w_expert_gate=w_expert_gate, w_expert_up=w_expert_up, w_expert_down=w_expert_down,
        ln2_g=ln2_g, ln2_b=ln2_b))
    y_p, k_p, v_p, ki_p = _prompt_step(x_prompt, p)
    y_s, k_s, v_s, ki_s, cv_s = _sample_step(x_sample, cache_k, cache_v, cache_kidx, page_table, p)
    return (y_p, y_s, k_p, v_p, ki_p, k_s, v_s, ki_s, cv_s)
```

```python
import functools

import jax
import jax.numpy as jnp
import numpy as np
from jax import lax
from jax.experimental import pallas as pl
from jax.experimental.pallas import tpu as pltpu

F32 = jnp.float32
BF16 = jnp.bfloat16

D_MODEL = 1024
N_HEADS = 8
HEAD_DIM = 64
ATTN_WIDTH = N_HEADS * HEAD_DIM
IDX_HEADS = 4
IDX_DIM = 64
TOPK = 256
ROPE_THETA = 500000.0
ROPE_ROT = HEAD_DIM // 4
ROPE_HALF = ROPE_ROT // 2
CHUNK = 128
GMLP_GROUPS = 4
GMLP_WIDTH = D_MODEL // 2
N_GROUPS = 4
EXPERTS_PER_GROUP = 8
N_EXPERTS = N_GROUPS * EXPERTS_PER_GROUP
EXPERT_DIM = D_MODEL // 4
PAGE_SIZE = 128
LN_EPS = 1e-5
DEEPNORM_ALPHA = 2.0 ** 0.25
MASK_BIAS = -1e30

VMEM_LIMIT_V7X = 56 * 1024 * 1024
LANES = 128

_ROW_Q = 0
_ROW_K = _ROW_Q + ATTN_WIDTH
_ROW_V = _ROW_K + ATTN_WIDTH
_ROW_QI = _ROW_V + ATTN_WIDTH
_ROW_KI = _ROW_QI + IDX_HEADS * IDX_DIM
_ROW_WI = _ROW_KI + IDX_DIM
_WI_PAD = 16
_ROWS_A = _ROW_WI + _WI_PAD
_V_ROWS = HEAD_DIM + 16
LOG2_E = 1.4426950408889634


def _cparams(sem):
    return pltpu.CompilerParams(dimension_semantics=sem, vmem_limit_bytes=VMEM_LIMIT_V7X)


def _layer_norm(x, g, b):
    mu = jnp.mean(x, axis=-1, keepdims=True)
    xc = x - mu
    var = jnp.mean(xc * xc, axis=-1, keepdims=True)
    return xc * lax.rsqrt(var + LN_EPS) * g + b


def _rope_rows(seg, cos, sin):
    x1 = seg[0:ROPE_HALF]
    x2 = seg[ROPE_HALF:ROPE_ROT]
    return jnp.concatenate([x1 * cos - x2 * sin, x2 * cos + x1 * sin, seg[ROPE_ROT:]], axis=0)


def _split_bf16(a):
    hi = a.astype(BF16)
    return hi, (a - hi.astype(F32)).astype(BF16)


def _proj_attn_kernel(x_ref, wt_ref, *rest, tk, q_scale, precise_idx):
    if precise_idx:
        wtlo_ref, rest = rest[0], rest[1:]
    (cos_ref, sin_ref,
     qz_ref, kn_ref, knb_ref, vn_ref, vtb_ref, qit_ref, kin_ref, kinb_ref, wit_ref) = rest
    x = x_ref[0]
    xb = x.astype(BF16)
    tm = xb.shape[0]
    cos = cos_ref[...]
    sin = sin_ref[...]
    nt = (((1,), (1,)), ((), ()))

    def proj_t(r0, n):
        return lax.dot_general(wt_ref[r0:r0 + n, :], xb, nt, preferred_element_type=F32)

    if precise_idx:
        xlo = (x - xb.astype(F32)).astype(BF16)

        def proj_idx(r0, n):
            w_hi = wt_ref[r0:r0 + n, :]
            return (lax.dot_general(w_hi, xb, nt, preferred_element_type=F32)
                    + lax.dot_general(w_hi, xlo, nt, preferred_element_type=F32)
                    + lax.dot_general(wtlo_ref[r0:r0 + n, :], xb, nt, preferred_element_type=F32))
    else:
        proj_idx = proj_t

    q_t = proj_t(_ROW_Q, ATTN_WIDTH)
    zeros = jnp.zeros((HEAD_DIM, tm), F32)
    for h in range(N_HEADS):
        rot = _rope_rows(q_t[h * HEAD_DIM:(h + 1) * HEAD_DIM], cos, sin) * q_scale
        pair = [rot, zeros] if h % 2 == 0 else [zeros, rot]
        qz_ref[0, h] = jnp.concatenate(pair, axis=0).astype(BF16)

    k_t = proj_t(_ROW_K, ATTN_WIDTH)
    k_rot = jnp.concatenate(
        [_rope_rows(k_t[h * HEAD_DIM:(h + 1) * HEAD_DIM], cos, sin) for h in range(N_HEADS)], axis=0)
    kn_ref[0] = k_rot
    knb_ref[0] = k_rot.T.astype(BF16)

    v_t = proj_t(_ROW_V, ATTN_WIDTH)
    vn_ref[0] = v_t
    ones = jnp.ones((_V_ROWS - HEAD_DIM, tk), F32)
    for c in range(tm // tk):
        v_c = v_t[:, c * tk:(c + 1) * tk]
        vtb_ref[0, c] = jnp.concatenate(
            [piece for h in range(N_HEADS) for piece in (v_c[h * HEAD_DIM:(h + 1) * HEAD_DIM], ones)],
            axis=0).astype(BF16)

    qi_t = proj_idx(_ROW_QI, IDX_HEADS * IDX_DIM)
    qit_ref[0] = jnp.concatenate(
        [_rope_rows(qi_t[h * IDX_DIM:(h + 1) * IDX_DIM], cos, sin) for h in range(IDX_HEADS)],
        axis=0).astype(qit_ref.dtype)

    ki_t = _rope_rows(proj_idx(_ROW_KI, IDX_DIM), cos, sin)
    kin_ref[0] = ki_t
    kinb_ref[0] = ki_t.T.astype(BF16)

    wit_ref[0] = proj_idx(_ROW_WI, _WI_PAD)[0:8]


def _proj_attn(x, wt_a, cos_t, sin_t, *, tm, tk, q_scale, wt_a_lo=None):
    b, s, d = x.shape
    grid = (b, s // tm)
    precise_idx = wt_a_lo is not None
    out_shape = (
        jax.ShapeDtypeStruct((b, N_HEADS, 2 * HEAD_DIM, s), BF16),
        jax.ShapeDtypeStruct((b, ATTN_WIDTH, s), F32),
        jax.ShapeDtypeStruct((b, s, ATTN_WIDTH), BF16),
        jax.ShapeDtypeStruct((b, ATTN_WIDTH, s), F32),
        jax.ShapeDtypeStruct((b, s // tk, N_HEADS * _V_ROWS, tk), BF16),
        jax.ShapeDtypeStruct((b, IDX_HEADS * IDX_DIM, s), F32 if precise_idx else BF16),
        jax.ShapeDtypeStruct((b, IDX_DIM, s), F32),
        jax.ShapeDtypeStruct((b, s, IDX_DIM), BF16),
        jax.ShapeDtypeStruct((b, 8, s), F32),
    )
    out_specs = (
        pl.BlockSpec((1, N_HEADS, 2 * HEAD_DIM, tm), lambda bi, i: (bi, 0, 0, i)),
        pl.BlockSpec((1, ATTN_WIDTH, tm), lambda bi, i: (bi, 0, i)),
        pl.BlockSpec((1, tm, ATTN_WIDTH), lambda bi, i: (bi, i, 0)),
        pl.BlockSpec((1, ATTN_WIDTH, tm), lambda bi, i: (bi, 0, i)),
        pl.BlockSpec((1, tm // tk, N_HEADS * _V_ROWS, tk), lambda bi, i: (bi, i, 0, 0)),
        pl.BlockSpec((1, IDX_HEADS * IDX_DIM, tm), lambda bi, i: (bi, 0, i)),
        pl.BlockSpec((1, IDX_DIM, tm), lambda bi, i: (bi, 0, i)),
        pl.BlockSpec((1, tm, IDX_DIM), lambda bi, i: (bi, i, 0)),
        pl.BlockSpec((1, 8, tm), lambda bi, i: (bi, 0, i)),
    )
    weights = [wt_a] + ([wt_a_lo] if precise_idx else [])
    in_specs = (
        [pl.BlockSpec((1, tm, d), lambda bi, i: (bi, i, 0))]
        + [pl.BlockSpec((_ROWS_A, d), lambda bi, i: (0, 0)) for _ in weights]
        + [pl.BlockSpec((ROPE_HALF, tm), lambda bi, i: (0, i)),
           pl.BlockSpec((ROPE_HALF, tm), lambda bi, i: (0, i))])
    return pl.pallas_call(
        functools.partial(_proj_attn_kernel, tk=tk, q_scale=q_scale, precise_idx=precise_idx),
        grid=grid, in_specs=in_specs, out_specs=out_specs, out_shape=out_shape,
        compiler_params=_cparams(("parallel", "parallel")), name="proj_attn",
    )(x, *weights, cos_t, sin_t)


def _gmlp_kernel(x_ref, w_ref, g_ref, b_ref, ws_ref, bst_ref, gm_ref, vn_ref, *, single_row_chunks):
    xb = x_ref[0].astype(BF16)
    tm = xb.shape[0]
    u = jnp.dot(xb, w_ref[:, :GMLP_WIDTH], preferred_element_type=F32)
    vb = jnp.dot(xb, w_ref[:, GMLP_WIDTH:], preferred_element_type=F32)
    vn = _layer_norm(vb, g_ref[...], b_ref[...])
    vn_ref[0] = vn
    gw = GMLP_WIDTH // GMLP_GROUPS
    row = lax.broadcasted_iota(jnp.int32, (CHUNK, CHUNK), 0)
    col = lax.broadcasted_iota(jnp.int32, (CHUNK, CHUNK), 1)
    if single_row_chunks:
        for g in range(GMLP_GROUPS):
            w00 = ws_ref[g][0:1, 0:1].astype(BF16).astype(F32)
            vg = vn[:, g * gw:(g + 1) * gw].astype(BF16).astype(F32)
            mix = vg * w00 + bst_ref[0:1, g:g + 1]
            gm_ref[0, :, g * gw:(g + 1) * gw] = (u[:, g * gw:(g + 1) * gw] * mix).astype(BF16)
    else:
        vnb = vn.astype(BF16)
        for g in range(GMLP_GROUPS):
            wsg = jnp.where(row >= col, ws_ref[g], 0.0).astype(BF16)
            bias = bst_ref[:, g:g + 1]
            for c in range(tm // CHUNK):
                mix = jnp.dot(wsg, vnb[c * CHUNK:(c + 1) * CHUNK, g * gw:(g + 1) * gw],
                              preferred_element_type=F32) + bias
                gm_ref[0, c * CHUNK:(c + 1) * CHUNK, g * gw:(g + 1) * gw] = (
                    u[c * CHUNK:(c + 1) * CHUNK, g * gw:(g + 1) * gw] * mix).astype(BF16)


def _gmlp(x, w_uv, ln_g, ln_b, w_spatial, b_spatial_t, *, tm, single_row_chunks):
    b, s, d = x.shape
    return pl.pallas_call(
        functools.partial(_gmlp_kernel, single_row_chunks=single_row_chunks),
        grid=(b, s // tm),
        in_specs=[
            pl.BlockSpec((1, tm, d), lambda bi, i: (bi, i, 0)),
            pl.BlockSpec((d, 2 * GMLP_WIDTH), lambda bi, i: (0, 0)),
            pl.BlockSpec((1, GMLP_WIDTH), lambda bi, i: (0, 0)),
            pl.BlockSpec((1, GMLP_WIDTH), lambda bi, i: (0, 0)),
            pl.BlockSpec((GMLP_GROUPS, CHUNK, CHUNK), lambda bi, i: (0, 0, 0)),
            pl.BlockSpec((CHUNK, GMLP_GROUPS), lambda bi, i: (0, 0)),
        ],
        out_specs=(pl.BlockSpec((1, tm, GMLP_WIDTH), lambda bi, i: (bi, i, 0)),
                   pl.BlockSpec((1, tm, GMLP_WIDTH), lambda bi, i: (bi, i, 0))),
        out_shape=(jax.ShapeDtypeStruct((b, s, GMLP_WIDTH), BF16),
                   jax.ShapeDtypeStruct((b, s, GMLP_WIDTH), F32)),
        compiler_params=_cparams(("parallel", "parallel")), name="gmlp",
    )(x, w_uv, ln_g, ln_b, w_spatial, b_spatial_t)


def _select_threshold(count_ge, count_gt, max_below, smin, smax, n_valid):
    k = float(TOPK)
    one = jnp.ones_like(smin)
    zero = jnp.zeros_like(smin)
    flag = lambda cnd: jnp.where(cnd, one, zero)
    live = flag(n_valid > k)
    lo, c_lo = smin, n_valid
    hi, c_hi = smax + (jnp.abs(smax) * 1e-6 + 1e-30), zero
    c0 = count_ge(zero)
    c0p = count_gt(zero)
    ge0 = flag(c0 >= k)
    raise_lo = live * ge0 * flag(zero > lo)
    lower_hi = live * (1.0 - ge0) * flag(zero < hi)
    lo, c_lo = jnp.where(raise_lo > 0.0, zero, lo), jnp.where(raise_lo > 0.0, c0, c_lo)
    hi, c_hi = jnp.where(lower_hi > 0.0, zero, hi), jnp.where(lower_hi > 0.0, c0, c_hi)
    live = live * (1.0 - ge0 * flag(c0p < k))
    log_k = float(np.log(k))

    def cond(st):
        _, _, c_lo, _, c_hi, live = st
        return jnp.sum(live * flag(c_lo - c_hi > 2.0)) > 0.0

    def body(st):
        it, lo, c_lo, hi, c_hi, live = st
        half = 0.5 * lo + 0.5 * hi
        log_lo = jnp.log(c_lo)
        frac = (log_lo - log_k) / jnp.maximum(log_lo - jnp.log(jnp.maximum(c_hi, 0.5)), 1e-6)
        interp = lo + (hi - lo) * jnp.clip(frac, 0.02, 0.98)
        inside = jnp.logical_and(interp > lo, interp < hi)
        mid = jnp.where(jnp.logical_and(it % 2 == 0, inside), interp, half)
        upd = live * flag(jnp.logical_and(half > lo, half < hi))
        c = count_ge(mid)
        ge = flag(c >= k)
        up, down = upd * ge > 0.0, upd * (1.0 - ge) > 0.0
        lo, c_lo = jnp.where(up, mid, lo), jnp.where(up, c, c_lo)
        hi, c_hi = jnp.where(down, mid, hi), jnp.where(down, c, c_hi)
        live = upd * flag(c != k)
        return it + 1, lo, c_lo, hi, c_hi, live

    _, lo, _, hi, _, live = lax.while_loop(cond, body, (jnp.int32(0), lo, c_lo, hi, c_hi, live))
    thr = jnp.where(live > 0.0, max_below(hi), lo)
    m = k - count_gt(thr)
    return thr, m


def _prompt_attn_kernel(qz_ref, knb_ref, vtb_ref, qit_ref, kinb_ref, wit_ref, out_ref,
                        sc_ref, acc_ref, *, tq):
    tk = tq
    i = pl.program_id(1)
    nkc = i + 1
    wi = wit_ref[0]
    inf = jnp.float32(jnp.inf)

    def raw_scores(c):
        r0 = pl.multiple_of(c * tk, tk)
        kic = kinb_ref[0, pl.ds(r0, tk), :]
        ds = [jnp.dot(kic, qit_ref[0, h * IDX_DIM:(h + 1) * IDX_DIM, :], preferred_element_type=F32)
              for h in range(IDX_HEADS)]
        s = jnp.maximum(ds[0], 0.0) * wi[0:1, :]
        for h in range(1, IDX_HEADS):
            s = s + jnp.maximum(ds[h], 0.0) * wi[h:h + 1, :]
        return r0, s

    def score_chunk(c, carry):
        smin, smax = carry
        r0, s = raw_scores(c)
        sc_ref[pl.ds(r0, tk), :] = s
        return (jnp.minimum(smin, jnp.min(s, axis=0, keepdims=True)),
                jnp.maximum(smax, jnp.max(s, axis=0, keepdims=True)))

    smin, smax = lax.fori_loop(0, i, score_chunk,
                               (jnp.full((1, tq), inf, F32), jnp.full((1, tq), -inf, F32)))
    r0, s = raw_scores(i)
    vis = lax.broadcasted_iota(jnp.int32, (tk, tq), 0) <= lax.broadcasted_iota(jnp.int32, (tk, tq), 1)
    s_vis = jnp.where(vis, s, -inf)
    sc_ref[pl.ds(r0, tk), :] = s_vis
    smax = jnp.maximum(smax, jnp.max(s_vis, axis=0, keepdims=True))
    smin = jnp.minimum(smin, jnp.min(jnp.where(vis, s, inf), axis=0, keepdims=True))

    @pl.when(nkc % 2 == 1)
    def _():
        sc_ref[pl.ds(pl.multiple_of(nkc * tk, tk), tk), :] = jnp.full((tk, tq), -inf, F32)

    def make_count(strict):
        def count(v):
            def body(c, acc):
                for half in range(2):
                    r0 = pl.multiple_of((2 * c + half) * tk, tk)
                    blk = sc_ref[pl.ds(r0, tk), :]
                    hit = (blk > v) if strict else (blk >= v)
                    acc = acc + jnp.sum(jnp.where(hit, 1.0, 0.0).reshape(tk // 8, 8, tq), axis=0)
                return acc
            acc = lax.fori_loop(0, (nkc + 1) // 2, body, jnp.zeros((8, tq), F32))
            return jnp.sum(acc, axis=0, keepdims=True)
        return count

    def max_below(v):
        def body(c, acc):
            blk = sc_ref[pl.ds(pl.multiple_of(c * tk, tk), tk), :]
            return jnp.maximum(acc, jnp.max(jnp.where(blk < v, blk, -inf).reshape(tk // 8, 8, tq), axis=0))
        acc = lax.fori_loop(0, nkc, body, jnp.full((8, tq), -inf, F32))
        return jnp.max(acc, axis=0, keepdims=True)

    n_valid = (i * tq + lax.broadcasted_iota(jnp.int32, (1, tq), 1) + 1).astype(F32)
    thr, m_take = _select_threshold(make_count(False), make_count(True), max_below, smin, smax, n_valid)

    acc_ref[...] = jnp.zeros_like(acc_ref)
    tri = (lax.broadcasted_iota(jnp.int32, (tk, tk), 0) >=
           lax.broadcasted_iota(jnp.int32, (tk, tk), 1)).astype(BF16)

    def attend_chunk(c, carry):
        tie_seen, ms, ls = carry
        r0 = pl.multiple_of(c * tk, tk)
        blk = sc_ref[pl.ds(r0, tk), :]
        tie = jnp.where(blk == thr, 1.0, 0.0).astype(BF16)
        rank = jnp.dot(tri, tie, preferred_element_type=F32) + tie_seen
        tie_rank = jnp.where(blk == thr, rank, inf)
        bias = jnp.where(blk > thr, 0.0, jnp.where(tie_rank <= m_take, 0.0, MASK_BIAS))
        kc = knb_ref[0, pl.ds(r0, tk), :]
        lgs = []
        for h in range(N_HEADS):
            k2 = kc[:, (h // 2) * 2 * HEAD_DIM:(h // 2 + 1) * 2 * HEAD_DIM]
            lgs.append(jnp.dot(k2, qz_ref[0, h], preferred_element_type=F32) + bias)
        new_ms = [jnp.maximum(ms[h], jnp.max(lgs[h], axis=0, keepdims=True)) for h in range(N_HEADS)]
        ps = [jnp.exp2(lgs[h] - new_ms[h]).astype(BF16) for h in range(N_HEADS)]
        pvs = [jnp.dot(vtb_ref[0, c, h * _V_ROWS:(h + 1) * _V_ROWS, :], ps[h], preferred_element_type=F32)
               for h in range(N_HEADS)]
        new_ls = []
        for h in range(N_HEADS):
            alpha = jnp.exp2(ms[h] - new_ms[h])
            acc_ref[h * HEAD_DIM:(h + 1) * HEAD_DIM, :] = (
                acc_ref[h * HEAD_DIM:(h + 1) * HEAD_DIM, :] * alpha + pvs[h][0:HEAD_DIM])
            new_ls.append(ls[h] * alpha + pvs[h][HEAD_DIM:HEAD_DIM + 1])
        return rank[tk - 1:tk, :], tuple(new_ms), tuple(new_ls)

    init = (jnp.zeros((1, tq), F32),
            tuple(jnp.full((1, tq), MASK_BIAS, F32) for _ in range(N_HEADS)),
            tuple(jnp.zeros((1, tq), F32) for _ in range(N_HEADS)))
    _, _, ls = lax.fori_loop(0, nkc, attend_chunk, init)

    out_t = jnp.concatenate(
        [acc_ref[h * HEAD_DIM:(h + 1) * HEAD_DIM, :] / ls[h] for h in range(N_HEADS)], axis=0)
    out_ref[0] = out_t.T.astype(BF16)


def _prompt_attn(qz, knb, vtb, qit, kinb, wit, *, tq):
    b, s, _ = knb.shape
    return pl.pallas_call(
        functools.partial(_prompt_attn_kernel, tq=tq),
        grid=(b, s // tq),
        in_specs=[
            pl.BlockSpec((1, N_HEADS, 2 * HEAD_DIM, tq), lambda bi, i: (bi, 0, 0, i)),
            pl.BlockSpec((1, s, ATTN_WIDTH), lambda bi, i: (bi, 0, 0), pipeline_mode=pl.Buffered(1)),
            pl.BlockSpec((1, s // tq, N_HEADS * _V_ROWS, tq), lambda bi, i: (bi, 0, 0, 0),
                         pipeline_mode=pl.Buffered(1)),
            pl.BlockSpec((1, IDX_HEADS * IDX_DIM, tq), lambda bi, i: (bi, 0, i)),
            pl.BlockSpec((1, s, IDX_DIM), lambda bi, i: (bi, 0, 0), pipeline_mode=pl.Buffered(1)),
            pl.BlockSpec((1, 8, tq), lambda bi, i: (bi, 0, i)),
        ],
        out_specs=pl.BlockSpec((1, tq, ATTN_WIDTH), lambda bi, i: (bi, i, 0)),
        out_shape=jax.ShapeDtypeStruct((b, s, ATTN_WIDTH), BF16),
        scratch_shapes=[
            pltpu.VMEM((s + tq, tq), F32),
            pltpu.VMEM((ATTN_WIDTH, tq), F32),
        ],
        compiler_params=_cparams(("parallel", "arbitrary")), name="prompt_attn",
    )(qz, knb, vtb, qit, kinb, wit)


_GROUP_LANE = N_EXPERTS + N_GROUPS


def _route(rl):
    lane = lax.broadcasted_iota(jnp.int32, rl.shape, 1).astype(F32)
    neg = jnp.float32(-jnp.inf)
    big = jnp.float32(1e9)
    g0 = float(N_EXPERTS)
    lg = jnp.where(jnp.logical_and(lane >= g0, lane < g0 + N_GROUPS), rl, neg)
    gmax = jnp.max(lg, axis=1, keepdims=True)
    gsel = jnp.min(jnp.where(lg == gmax, lane, big), axis=1, keepdims=True) - g0
    pgsel = 1.0 / jnp.sum(jnp.exp(lg - gmax), axis=1, keepdims=True)
    e0 = gsel * EXPERTS_PER_GROUP
    le = jnp.where(jnp.logical_and(lane >= e0, lane < e0 + EXPERTS_PER_GROUP), rl, neg)
    v1 = jnp.max(le, axis=1, keepdims=True)
    i1 = jnp.min(jnp.where(le == v1, lane, big), axis=1, keepdims=True)
    le2 = jnp.where(lane == i1, neg, le)
    v2 = jnp.max(le2, axis=1, keepdims=True)
    i2 = jnp.min(jnp.where(le2 == v2, lane, big), axis=1, keepdims=True)
    ex = jnp.exp(v2 - v1)
    den = 1.0 + ex
    p1 = pgsel / den
    p2 = pgsel * ex / den
    comb = jnp.where(lane == i1, p1, jnp.where(lane == i2, p2, 0.0))
    return jnp.where(lane == float(_GROUP_LANE), gsel, comb)


def _merge_kernel(x_ref, attn_ref, gm_ref, wgate_ref, gbias_ref, wa_ref, wb_ref, wo_ref,
                  lng_ref, lnb_ref, wr_ref, br_ref, hr_ref, *, precise_router):
    x = x_ref[0]
    xb = x.astype(BF16)
    ga = jax.nn.sigmoid(jnp.dot(xb, wgate_ref[:, :D_MODEL], preferred_element_type=F32) + gbias_ref[0:1, :])
    gb = jax.nn.sigmoid(jnp.dot(xb, wgate_ref[:, D_MODEL:], preferred_element_type=F32) + gbias_ref[1:2, :])
    a = jnp.dot(attn_ref[0], wa_ref[...], preferred_element_type=F32)
    bm = jnp.dot(gm_ref[0], wb_ref[...], preferred_element_type=F32)
    merged = ga * a + gb * bm
    y1 = DEEPNORM_ALPHA * x + jnp.dot(merged.astype(BF16), wo_ref[...], preferred_element_type=F32)
    h = _layer_norm(y1, lng_ref[...], lnb_ref[...])
    hr_ref[0, :, 0:D_MODEL] = h
    if precise_router:
        h_hi, h_lo = _split_bf16(h)
        w_hi, w_lo = _split_bf16(wr_ref[...])
        rl = (jnp.dot(h_hi, w_hi, preferred_element_type=F32) + jnp.dot(h_lo, w_hi, preferred_element_type=F32)
              + jnp.dot(h_hi, w_lo, preferred_element_type=F32))
    else:
        rl = jnp.dot(h.astype(BF16), wr_ref[...].astype(BF16), preferred_element_type=F32)
    hr_ref[0, :, D_MODEL:] = _route(rl + br_ref[...])


def _merge(x, attn, gm, p, *, tm, precise_router):
    b, s, d = x.shape
    const = lambda shape: pl.BlockSpec(shape, lambda bi, i: tuple(0 for _ in shape))
    return pl.pallas_call(
        functools.partial(_merge_kernel, precise_router=precise_router),
        grid=(b, s // tm),
        in_specs=[
            pl.BlockSpec((1, tm, d), lambda bi, i: (bi, i, 0)),
            pl.BlockSpec((1, tm, ATTN_WIDTH), lambda bi, i: (bi, i, 0)),
            pl.BlockSpec((1, tm, GMLP_WIDTH), lambda bi, i: (bi, i, 0)),
            const((d, 2 * d)), const((2, d)), const((ATTN_WIDTH, d)), const((GMLP_WIDTH, d)), const((d, d)),
            const((1, d)), const((1, d)), const((d, LANES)), const((1, LANES)),
        ],
        out_specs=pl.BlockSpec((1, tm, d + LANES), lambda bi, i: (bi, i, 0)),
        out_shape=jax.ShapeDtypeStruct((b, s, d + LANES), F32),
        compiler_params=_cparams(("parallel", "parallel")), name="merge_ln1_route",
    )(x, attn, gm, p["w_gates"], p["gate_bias"], p["w_branch_a"], p["w_branch_b"], p["w_o"],
      p["ln1_g"], p["ln1_b"], p["w_router"], p["b_router"])


def _moe_kernel(h_ref, comb_ref, wg_ref, wu_ref, wd_ref, lng_ref, lnb_ref, y_ref, acc_ref, hb_ref):
    e = pl.program_id(1)

    @pl.when(e == 0)
    def _():
        acc_ref[...] = jnp.zeros_like(acc_ref)
        hb_ref[...] = h_ref[...].astype(BF16)

    comb = comb_ref[...]
    lane = lax.broadcasted_iota(jnp.int32, comb.shape, 1)
    wcol = jnp.sum(jnp.where(lane == e, comb, 0.0), axis=1, keepdims=True)
    tm = hb_ref.shape[0]
    halves = [slice(r, r + tm // 2) for r in (0, tm // 2)] if tm % 16 == 0 else [slice(0, tm)]
    gates = [jnp.dot(hb_ref[rows, :], wg_ref[0], preferred_element_type=F32) for rows in halves]
    ups = [jnp.dot(hb_ref[rows, :], wu_ref[0], preferred_element_type=F32) for rows in halves]
    for rows, gate, up in zip(halves, gates, ups):
        hg = jax.nn.silu(gate) * up * wcol[rows]
        acc_ref[rows, :] += jnp.dot(hg.astype(BF16), wd_ref[0], preferred_element_type=F32)

    @pl.when(e == pl.num_programs(1) - 1)
    def _():
        y_ref[...] = _layer_norm(DEEPNORM_ALPHA * h_ref[...] + acc_ref[...], lng_ref[...], lnb_ref[...])


def _moe(hr, p, *, tm):
    t, d = hr.shape[0], hr.shape[1] - LANES
    return pl.pallas_call(
        _moe_kernel,
        grid=(t // tm, N_EXPERTS),
        in_specs=[
            pl.BlockSpec((tm, d), lambda i, e: (i, 0)),
            pl.BlockSpec((tm, LANES), lambda i, e: (i, d // LANES)),
            pl.BlockSpec((1, d, EXPERT_DIM), lambda i, e: (e, 0, 0)),
            pl.BlockSpec((1, d, EXPERT_DIM), lambda i, e: (e, 0, 0)),
            pl.BlockSpec((1, EXPERT_DIM, d), lambda i, e: (e, 0, 0)),
            pl.BlockSpec((1, d), lambda i, e: (0, 0)),
            pl.BlockSpec((1, d), lambda i, e: (0, 0)),
        ],
        out_specs=pl.BlockSpec((tm, d), lambda i, e: (i, 0)),
        out_shape=jax.ShapeDtypeStruct((t, d), F32),
        scratch_shapes=[pltpu.VMEM((tm, d), F32), pltpu.VMEM((tm, d), BF16)],
        compiler_params=_cparams(("parallel", "arbitrary")), name="moe_ln2",
    )(hr, hr, p["w_expert_gate"], p["w_expert_up"], p["w_expert_down"], p["ln2_g"], p["ln2_b"])


_MOE_TILE = 512


def _moe_sorted_kernel(tg_ref, nreal_ref, src_ref, hr_hbm, wg_ref, wu_ref, wd_ref, lng_ref, lnb_ref, y_hbm,
                       hbuf, hb_ref, acc_ref, ybuf, gsem, ssem):
    i = pl.program_id(0)
    e = pl.program_id(1)
    n_tiles = pl.num_programs(0)
    tm = hb_ref.shape[0]
    d = hb_ref.shape[1]
    slot = i % 2

    def start_gather(tile, slot_):
        def body(r, carry):
            pltpu.make_async_copy(hr_hbm.at[src_ref[tile * tm + r]], hbuf.at[slot_, r], gsem.at[slot_]).start()
            return carry
        lax.fori_loop(0, tm, body, 0, unroll=8)

    def wait_gather(slot_):
        pltpu.make_async_copy(hbuf.at[slot_], hbuf.at[slot_], gsem.at[slot_]).wait()

    def start_write_back(tile, slot_):
        def start_row(r):
            pltpu.make_async_copy(ybuf.at[slot_, r], y_hbm.at[src_ref[tile * tm + r]], ssem.at[slot_]).start()

        def body8(j, carry):
            for u in range(8):
                start_row(j * 8 + u)
            return carry

        def body1(r, carry):
            start_row(r)
            return carry
        n = nreal_ref[tile]
        lax.fori_loop(0, n // 8, body8, 0)
        lax.fori_loop((n // 8) * 8, n, body1, 0)

    def wait_write_back(tile, slot_):
        n = nreal_ref[tile]
        n_tiled = pl.multiple_of((n // 8) * 8, 8)

        @pl.when(n_tiled > 0)
        def _():
            rows = ybuf.at[slot_, pl.ds(0, n_tiled)]
            pltpu.make_async_copy(rows, rows, ssem.at[slot_]).wait()

        def body(r, carry):
            pltpu.make_async_copy(ybuf.at[slot_, r], ybuf.at[slot_, r], ssem.at[slot_]).wait()
            return carry
        lax.fori_loop(n_tiled, n, body, 0)

    @pl.when(e == 0)
    def _():
        @pl.when(i == 0)
        def _():
            start_gather(0, 0)
        wait_gather(slot)

        @pl.when(i + 1 < n_tiles)
        def _():
            start_gather(i + 1, 1 - slot)
        hb_ref[...] = hbuf[slot, :, 0:d].astype(BF16)
        acc_ref[...] = jnp.zeros_like(acc_ref)

    comb = hbuf[slot, :, d:]
    lane = lax.broadcasted_iota(jnp.int32, comb.shape, 1)
    expert = tg_ref[i] * EXPERTS_PER_GROUP + e
    wcol = jnp.sum(jnp.where(lane == expert, comb, 0.0), axis=1, keepdims=True)
    halves = [slice(r, r + tm // 2) for r in (0, tm // 2)]
    gates = [jnp.dot(hb_ref[rows, :], wg_ref[0], preferred_element_type=F32) for rows in halves]
    ups = [jnp.dot(hb_ref[rows, :], wu_ref[0], preferred_element_type=F32) for rows in halves]
    for rows, gate, up in zip(halves, gates, ups):
        hg = jax.nn.silu(gate) * up * wcol[rows]
        acc_ref[rows, :] += jnp.dot(hg.astype(BF16), wd_ref[0], preferred_element_type=F32)

    @pl.when(e == pl.num_programs(1) - 1)
    def _():
        @pl.when(i >= 2)
        def _():
            wait_write_back(i - 2, slot)
        ybuf[slot] = _layer_norm(DEEPNORM_ALPHA * hbuf[slot, :, 0:d] + acc_ref[...], lng_ref[...], lnb_ref[...])
        start_write_back(i, slot)

        @pl.when(i == n_tiles - 1)
        def _():
            @pl.when(i >= 1)
            def _():
                wait_write_back(i - 1, 1 - slot)
            wait_write_back(i, slot)


def _moe_sorted(hr, p):
    t, d = hr.shape[0], hr.shape[1] - LANES
    tm = _MOE_TILE
    assert t % tm == 0
    grp = hr[:, d + _GROUP_LANE].astype(jnp.int32)
    order = jnp.argsort(grp, stable=True).astype(jnp.int32)
    counts = jnp.sum(grp[:, None] == jnp.arange(N_GROUPS, dtype=jnp.int32)[None, :], axis=0).astype(jnp.int32)
    tiles = (counts + tm - 1) // tm
    tile_end = jnp.cumsum(tiles)
    tile_start = tile_end - tiles
    grp_start = jnp.cumsum(counts) - counts
    n_tiles = t // tm + N_GROUPS
    tile_id = jnp.arange(n_tiles, dtype=jnp.int32)
    tile_group = jnp.minimum(jnp.sum(tile_id[:, None] >= tile_end[None, :], axis=1),
                             N_GROUPS - 1).astype(jnp.int32)
    rows = jnp.arange(n_tiles * tm, dtype=jnp.int32)
    row_group = tile_group[rows // tm]
    k = rows - tile_start[row_group] * tm
    src = order[jnp.clip(grp_start[row_group] + jnp.minimum(k, counts[row_group] - 1), 0, t - 1)]
    n_real = jnp.clip(counts[tile_group] - (tile_id - tile_start[tile_group]) * tm, 0, tm).astype(jnp.int32)
    n_real = jnp.where(tile_id < tile_end[-1], n_real, 0)

    expert_block = lambda shape: pl.BlockSpec(
        shape, lambda i, e, tg, nr, sr: (tg[i] * EXPERTS_PER_GROUP + e, 0, 0))
    const_row = pl.BlockSpec((1, d), lambda i, e, tg, nr, sr: (0, 0))
    return pl.pallas_call(
        _moe_sorted_kernel,
        grid_spec=pltpu.PrefetchScalarGridSpec(
            num_scalar_prefetch=3, grid=(n_tiles, EXPERTS_PER_GROUP),
            in_specs=[pl.BlockSpec(memory_space=pl.ANY),
                      expert_block((1, d, EXPERT_DIM)), expert_block((1, d, EXPERT_DIM)),
                      expert_block((1, EXPERT_DIM, d)), const_row, const_row],
            out_specs=pl.BlockSpec(memory_space=pl.ANY),
            scratch_shapes=[pltpu.VMEM((2, tm, d + LANES), F32),
                            pltpu.VMEM((tm, d), BF16), pltpu.VMEM((tm, d), F32), pltpu.VMEM((2, tm, d), F32),
                            pltpu.SemaphoreType.DMA((2,)), pltpu.SemaphoreType.DMA((2,))],
        ),
        out_shape=jax.ShapeDtypeStruct((t, d), F32),
        compiler_params=_cparams(("arbitrary", "arbitrary")), name="moe_sorted_ln2",
    )(tile_group, n_real, src.astype(jnp.int32), hr, p["w_expert_gate"], p["w_expert_up"], p["w_expert_down"],
      p["ln2_g"], p["ln2_b"])


_PAGES_PER_STEP = 8


def _page_specs(block, n):
    zeros = (0,) * (len(block) - 1)

    def spec(r):
        return pl.BlockSpec(block, lambda b, j, pt: (pt[b, j * n + r],) + zeros)
    return [spec(r) for r in range(n)]


def _sample_scores_kernel(pt_ref, qi_ref, wi_ref, *refs):
    page_refs, out_ref = refs[:-1], refs[-1]
    q_hi, q_lo = _split_bf16(qi_ref[0])
    q2 = jnp.concatenate([q_hi, q_lo], axis=0)
    wi = wi_ref[0][:, 0:1]
    pages = page_refs[0][0] if len(page_refs) == 1 else jnp.concatenate([p[0] for p in page_refs], axis=1)
    k_hi, k_lo = _split_bf16(pages)
    d2 = jnp.dot(q2, k_hi, preferred_element_type=F32)
    d = d2[0:8] + d2[8:16] + jnp.dot(q_hi, k_lo, preferred_element_type=F32)
    s = jnp.sum(jnp.maximum(d, 0.0) * wi, axis=0, keepdims=True)
    for r in range(len(page_refs)):
        out_ref[0, r:r + 1, :] = s[:, r * PAGE_SIZE:(r + 1) * PAGE_SIZE]


def _sample_scores(page_table, qi_s, wi_s, kidx_t, *, n):
    db, n_pages = page_table.shape
    return pl.pallas_call(
        _sample_scores_kernel,
        grid_spec=pltpu.PrefetchScalarGridSpec(
            num_scalar_prefetch=1, grid=(db, n_pages // n),
            in_specs=[pl.BlockSpec((1, 8, IDX_DIM), lambda b, j, pt: (b, 0, 0)),
                      pl.BlockSpec((1, 8, LANES), lambda b, j, pt: (b, 0, 0))]
            + _page_specs((1, IDX_DIM, PAGE_SIZE), n),
            out_specs=pl.BlockSpec((1, n, PAGE_SIZE), lambda b, j, pt: (b, j, 0)),
        ),
        out_shape=jax.ShapeDtypeStruct((db, n_pages, PAGE_SIZE), F32),
        compiler_params=_cparams(("parallel", "arbitrary")), name="sample_scores",
    )(page_table, qi_s, wi_s, *([kidx_t] * n))


def _sample_select_kernel(sc_ref, nvalid_ref, sel_ref, *, tk):
    nk, tq = sc_ref.shape
    nkc = nk // tk

    def chunk_reduce(fn, init):
        def body(c, acc):
            r0 = pl.multiple_of(c * tk, tk)
            return fn(acc, sc_ref[pl.ds(r0, tk), :])
        return lax.fori_loop(0, nkc, body, init)

    inf = jnp.float32(jnp.inf)
    smax = chunk_reduce(lambda a, blk: jnp.maximum(a, jnp.max(blk, axis=0, keepdims=True)),
                        jnp.full((1, tq), -inf, F32))
    smin = chunk_reduce(
        lambda a, blk: jnp.minimum(a, jnp.min(jnp.where(blk == -inf, inf, blk), axis=0, keepdims=True)),
        jnp.full((1, tq), inf, F32))

    def make_count(strict):
        def count(v):
            def fn(acc, blk):
                ind = jnp.where((blk > v) if strict else (blk >= v), 1.0, 0.0)
                return acc + jnp.sum(ind.reshape(tk // 8, 8, tq), axis=0)
            return jnp.sum(chunk_reduce(fn, jnp.zeros((8, tq), F32)), axis=0, keepdims=True)
        return count

    def max_below(v):
        fn = lambda acc, blk: jnp.maximum(
            acc, jnp.max(jnp.where(blk < v, blk, -inf).reshape(tk // 8, 8, tq), axis=0))
        return jnp.max(chunk_reduce(fn, jnp.full((8, tq), -inf, F32)), axis=0, keepdims=True)

    thr, m_take = _select_threshold(make_count(False), make_count(True), max_below, smin, smax,
                                    nvalid_ref[...])
    tri = (lax.broadcasted_iota(jnp.int32, (tk, tk), 0) >=
           lax.broadcasted_iota(jnp.int32, (tk, tk), 1)).astype(BF16)

    def emit(c, tie_seen):
        r0 = pl.multiple_of(c * tk, tk)
        blk = sc_ref[pl.ds(r0, tk), :]
        tie = jnp.where(blk == thr, 1.0, 0.0)
        rank = jnp.dot(tri, tie.astype(BF16), preferred_element_type=F32) + tie_seen
        keep_tie = tie * jnp.where(rank <= m_take, 1.0, 0.0)
        sel_ref[pl.ds(r0, tk), :] = jnp.where(blk > thr, 1.0, keep_tie)
        return tie_seen + jnp.sum(tie, axis=0, keepdims=True)

    lax.fori_loop(0, nkc, emit, jnp.zeros((1, tq), F32))


def _sample_select(sc_t, n_valid, *, tk):
    nk, tq = sc_t.shape
    return pl.pallas_call(
        functools.partial(_sample_select_kernel, tk=tk),
        grid=(1,),
        in_specs=[pl.BlockSpec((nk, tq), lambda i: (0, 0)), pl.BlockSpec((1, tq), lambda i: (0, 0))],
        out_specs=pl.BlockSpec((nk, tq), lambda i: (0, 0)),
        out_shape=jax.ShapeDtypeStruct((nk, tq), F32),
        compiler_params=_cparams(("arbitrary",)), name="sample_select",
    )(sc_t, n_valid)


def _sample_attend_kernel(pt_ref, qb_ref, sel_ref, knew_ref, vnew_ref, selnew_ref, *refs, n):
    k_refs, v_refs = refs[:n], refs[n:2 * n]
    out_ref, acc_ref, m_ref, l_ref = refs[2 * n:]
    j = pl.program_id(1)
    lane = lax.broadcasted_iota(jnp.int32, (1, PAGE_SIZE), 1)

    def update(h, lgs, vals):
        m_old = m_ref[h:h + 1, :]
        m_new = m_old
        for lg in lgs:
            m_new = jnp.maximum(m_new, jnp.max(lg, axis=1, keepdims=True))
        alpha = jnp.exp(m_old - m_new)
        l_new = alpha * l_ref[h:h + 1, :]
        acc = acc_ref[h] * alpha
        for lg, val in zip(lgs, vals):
            p = jnp.exp(lg - m_new)
            l_new = l_new + jnp.sum(p, axis=1, keepdims=True)
            acc = acc + val * p
        m_ref[h:h + 1, :] = m_new
        l_ref[h:h + 1, :] = l_new
        acc_ref[h] = acc

    @pl.when(j == 0)
    def _():
        m_ref[...] = jnp.full_like(m_ref, MASK_BIAS)
        l_ref[...] = jnp.zeros_like(l_ref)
        acc_ref[...] = jnp.zeros_like(acc_ref)
        for h in range(N_HEADS):
            lg = jnp.sum(knew_ref[0, h] * qb_ref[0, h], axis=0, keepdims=True)
            keep = jnp.logical_and(lane == 0, selnew_ref[0, h:h + 1, :] > 0.0)
            update(h, [jnp.where(keep, lg, MASK_BIAS)], [vnew_ref[0, h]])

    biases = [jnp.where(sel_ref[0, r:r + 1, :] > 0.0, 0.0, MASK_BIAS) for r in range(n)]
    for h in range(N_HEADS):
        q_h = qb_ref[0, h]
        lgs = [jnp.sum(k_refs[r][0, h] * q_h, axis=0, keepdims=True) + biases[r] for r in range(n)]
        update(h, lgs, [v_refs[r][0, h] for r in range(n)])

    @pl.when(j == pl.num_programs(1) - 1)
    def _():
        for h in range(N_HEADS):
            out_ref[0, h] = jnp.sum(acc_ref[h], axis=1, keepdims=True) / l_ref[h:h + 1, 0:1]


def _sample_attend(page_table, qb, sel, k_new, v_new, sel_new, ck_t, cv_t):
    db, n_pages = page_table.shape
    n = _PAGES_PER_STEP
    head_tile = (1, N_HEADS, HEAD_DIM, PAGE_SIZE)
    per_sample = pl.BlockSpec(head_tile, lambda b, j, pt: (b, 0, 0, 0))
    return pl.pallas_call(
        functools.partial(_sample_attend_kernel, n=n),
        grid_spec=pltpu.PrefetchScalarGridSpec(
            num_scalar_prefetch=1, grid=(db, n_pages // n),
            in_specs=[per_sample,
                      pl.BlockSpec((1, n, PAGE_SIZE), lambda b, j, pt: (b, j, 0)),
                      per_sample, per_sample,
                      pl.BlockSpec((1, N_HEADS, PAGE_SIZE), lambda b, j, pt: (b, 0, 0))]
            + _page_specs(head_tile, n) + _page_specs(head_tile, n),
            out_specs=pl.BlockSpec((1, N_HEADS, HEAD_DIM, 1), lambda b, j, pt: (b, 0, 0, 0)),
            scratch_shapes=[pltpu.VMEM((N_HEADS, HEAD_DIM, PAGE_SIZE), F32),
                            pltpu.VMEM((N_HEADS, PAGE_SIZE), F32), pltpu.VMEM((N_HEADS, PAGE_SIZE), F32)],
        ),
        out_shape=jax.ShapeDtypeStruct((db, N_HEADS, HEAD_DIM, 1), F32),
        compiler_params=_cparams(("parallel", "arbitrary")), name="sample_attend",
    )(page_table, qb, sel, k_new, v_new, sel_new, *([ck_t] * n), *([cv_t] * n))


def _rope_tables(pos):
    inv_freq = jnp.power(jnp.float32(ROPE_THETA), -jnp.arange(0, ROPE_ROT, 2, dtype=F32) / ROPE_ROT)
    ang = pos.astype(F32)[:, None] * inv_freq[None, :]
    return jnp.cos(ang).T, jnp.sin(ang).T


def _prepare_weights(p):
    w_in = p["w_in"]
    sizes = [ATTN_WIDTH, ATTN_WIDTH, ATTN_WIDTH, IDX_HEADS * IDX_DIM, IDX_DIM, IDX_HEADS,
             GMLP_WIDTH, GMLP_WIDTH, D_MODEL, D_MODEL]
    offs = np.concatenate([[0], np.cumsum(sizes)])
    n_attn = int(offs[6])
    wt_a_f32 = jnp.pad(w_in[:, :n_attn].T, ((0, _ROWS_A - n_attn), (0, 0)))
    wt_a = wt_a_f32.astype(BF16)
    wt_a_lo = (wt_a_f32 - wt_a.astype(F32)).astype(BF16)
    w_uv = w_in[:, int(offs[6]):int(offs[8])].astype(BF16)
    w_gates = w_in[:, int(offs[8]):].astype(BF16)
    w_re = jnp.transpose(p["w_router_expert"], (1, 0, 2)).reshape(D_MODEL, N_EXPERTS)
    w_router = jnp.pad(jnp.concatenate([w_re, p["w_router_group"]], axis=1),
                       ((0, 0), (0, LANES - N_EXPERTS - N_GROUPS)))
    b_router = jnp.pad(jnp.concatenate([p["b_router_expert"].reshape(-1), p["b_router_group"]]),
                       (0, LANES - N_EXPERTS - N_GROUPS)).reshape(1, LANES)
    return {
        "wt_a": wt_a, "wt_a_lo": wt_a_lo, "w_uv": w_uv, "w_gates": w_gates,
        "gate_bias": p["gate_bias"],
        "ln_v_g": p["ln_v_g"].reshape(1, -1), "ln_v_b": p["ln_v_b"].reshape(1, -1),
        "w_spatial": p["w_spatial"], "b_spatial_t": p["b_spatial"].T,
        "w_branch_a": p["w_branch_a"].astype(BF16), "w_branch_b": p["w_branch_b"].astype(BF16),
        "w_o": p["w_o"].astype(BF16),
        "ln1_g": p["ln1_g"].reshape(1, -1), "ln1_b": p["ln1_b"].reshape(1, -1),
        "ln2_g": p["ln2_g"].reshape(1, -1), "ln2_b": p["ln2_b"].reshape(1, -1),
        "w_router": w_router, "b_router": b_router,
        "w_expert_gate": p["w_expert_gate"].reshape(N_EXPERTS, D_MODEL, EXPERT_DIM).astype(BF16),
        "w_expert_up": p["w_expert_up"].reshape(N_EXPERTS, D_MODEL, EXPERT_DIM).astype(BF16),
        "w_expert_down": p["w_expert_down"].reshape(N_EXPERTS, EXPERT_DIM, D_MODEL).astype(BF16),
    }


def _pick_tile(n, pref):
    t = min(n, pref)
    assert n % t == 0, (n, pref)
    return t


def _channel_mix(x, attn, gm, p, *, precise_router):
    b, s, d = x.shape
    t = b * s
    hr = _merge(x, attn, gm, p, tm=_pick_tile(s, 512), precise_router=precise_router).reshape(t, d + LANES)
    if t % _MOE_TILE == 0 and t >= N_GROUPS * _MOE_TILE:
        y = _moe_sorted(hr, p)
    else:
        y = _moe(hr, p, tm=_pick_tile(t, 1024))
    return y.reshape(b, s, d)


def _prompt_step(x, p):
    b, s, _ = x.shape
    cos_t, sin_t = _rope_tables(jnp.arange(s, dtype=jnp.int32))
    tq = 256
    qz, kn, knb, vn, vtb, qit, kin, kinb, wit = _proj_attn(
        x, p["wt_a"], cos_t, sin_t, tm=512, tk=tq, q_scale=HEAD_DIM ** -0.5 * LOG2_E)
    attn = _prompt_attn(qz, knb, vtb, qit, kinb, wit, tq=tq)
    gm, _ = _gmlp(x, p["w_uv"], p["ln_v_g"], p["ln_v_b"], p["w_spatial"], p["b_spatial_t"],
                  tm=512, single_row_chunks=False)
    y = _channel_mix(x, attn, gm, p, precise_router=False)
    return y, _rows_from_t(kn), _rows_from_t(vn), jnp.transpose(kin, (0, 2, 1))


def _rows_from_t(a_t):
    b, _, s = a_t.shape
    return jnp.transpose(a_t.reshape(b, N_HEADS, HEAD_DIM, s), (0, 3, 1, 2))


def _sample_step(x, cache_k, cache_v, cache_kidx, page_table, p):
    db, t_new, d = x.shape
    assert t_new == 1
    n_pages = page_table.shape[1]
    past = n_pages * PAGE_SIZE
    rows = LANES
    assert db <= rows
    xs = jnp.pad(x.reshape(1, db, d), ((0, 0), (0, rows - db), (0, 0)))
    cos_t, sin_t = _rope_tables(jnp.full((rows,), past, jnp.int32))
    qz, kn, knb, vn, vtb, qit, kin, kinb, wit = _proj_attn(
        xs, p["wt_a"], cos_t, sin_t, tm=rows, tk=rows, q_scale=HEAD_DIM ** -0.5, wt_a_lo=p["wt_a_lo"])
    gm, chunk_v = _gmlp(xs, p["w_uv"], p["ln_v_g"], p["ln_v_b"], p["w_spatial"], p["b_spatial_t"],
                        tm=rows, single_row_chunks=True)

    qi_s = jnp.pad(qit[0].T.reshape(rows, IDX_HEADS, IDX_DIM), ((0, 0), (0, 8 - IDX_HEADS), (0, 0)))[:db]
    wi_s = jnp.broadcast_to(wit[0].T[:db, :, None], (db, 8, LANES))
    kidx_t = jnp.transpose(cache_kidx, (0, 2, 1))
    sc_past = _sample_scores(page_table, qi_s, wi_s, kidx_t, n=_PAGES_PER_STEP)
    own_page = jnp.pad(kin[0, :, :db].T[:, :, None], ((0, 0), (0, 0), (0, PAGE_SIZE - 1)))
    own_table = jnp.arange(db, dtype=jnp.int32).reshape(db, 1)
    sc_new = _sample_scores(own_table, qi_s, wi_s, own_page, n=1)[:, 0, 0]
    tk = 256
    nk = past + tk
    sc_all = jnp.concatenate([sc_past.reshape(db, past), sc_new[:, None],
                              jnp.full((db, tk - 1), -jnp.inf, F32)], axis=1)
    sc_t = jnp.pad(sc_all.T, ((0, 0), (0, rows - db)))
    n_valid = jnp.where(jnp.arange(rows) < db, float(past + 1), 0.0).astype(F32).reshape(1, rows)
    sel_t = _sample_select(sc_t, n_valid, tk=tk)
    sel = sel_t.T[:db]
    sel_past = sel[:, :past].reshape(db, n_pages, PAGE_SIZE)
    sel_new = jnp.broadcast_to(sel[:, past][:, None, None], (db, N_HEADS, PAGE_SIZE))

    def lane_replicated(a):
        return jnp.broadcast_to(a.astype(F32)[..., None], a.shape + (PAGE_SIZE,))

    q_s = jnp.stack([qz[0, hh, (hh % 2) * HEAD_DIM:(hh % 2 + 1) * HEAD_DIM, :db].T for hh in range(N_HEADS)],
                    axis=1)
    k_new = kn[0, :, :db].T.reshape(db, N_HEADS, HEAD_DIM)
    v_new = vn[0, :, :db].T.reshape(db, N_HEADS, HEAD_DIM)
    attn = _sample_attend(page_table, lane_replicated(q_s), sel_past, lane_replicated(k_new),
                          lane_replicated(v_new), sel_new,
                          jnp.transpose(cache_k, (0, 2, 3, 1)), jnp.transpose(cache_v, (0, 2, 3, 1)))
    attn_p = jnp.pad(attn.reshape(1, db, ATTN_WIDTH), ((0, 0), (0, rows - db), (0, 0))).astype(BF16)
    y = _channel_mix(xs, attn_p, gm, p, precise_router=True)
    return (y[0, :db].reshape(db, 1, d),
            k_new.reshape(db, 1, N_HEADS, HEAD_DIM), v_new.reshape(db, 1, N_HEADS, HEAD_DIM),
            kin[0, :, :db].T.reshape(db, 1, IDX_DIM), chunk_v[0, :db].reshape(db, 1, GMLP_WIDTH))


def kernel(x_prompt, x_sample, cache_k, cache_v, cache_kidx, page_table, w_in, gate_bias,
           ln_v_g, ln_v_b, w_spatial, b_spatial, w_branch_a, w_branch_b, w_o, ln1_g, ln1_b,
           w_router_group, b_router_group, w_router_expert, b_router_expert,
           w_expert_gate, w_expert_up, w_expert_down, ln2_g, ln2_b):
    p = _prepare_weights(dict(
        w_in=w_in, gate_bias=gate_bias, ln_v_g=ln_v_g, ln_v_b=ln_v_b, w_spatial=w_spatial,
        b_spatial=b_spatial, w_branch_a=w_branch_a, w_branch_b=w_branch_b, w_o=w_o, ln1_g=ln1_g,
        ln1_b=ln1_b, w_router_group=w_router_group, b_router_group=b_router_group,
        w_router_expert=w_router_expert, b_router_expert=b_router_expert,
        w_expert_gate=w_expert_gate, w_expert_up=w_expert_up, w_expert_down=w_expert_down,
        ln2_g=ln2_g, ln2_b=ln2_b))
    y_p, k_p, v_p, ki_p = _prompt_step(x_prompt, p)
    y_s, k_s, v_s, ki_s, cv_s = _sample_step(x_sample, cache_k, cache_v, cache_kidx, page_table, p)
    return (y_p, y_s, k_p, v_p, ki_p, k_s, v_s, ki_s, cv_s)
```

```python
import functools

import jax
import jax.numpy as jnp
import numpy as np
from jax import lax
from jax.experimental import pallas as pl
from jax.experimental.pallas import tpu as pltpu

F32 = jnp.float32
BF16 = jnp.bfloat16

D_MODEL = 1024
N_HEADS = 8
HEAD_DIM = 64
ATTN_WIDTH = N_HEADS * HEAD_DIM
IDX_HEADS = 4
IDX_DIM = 64
TOPK = 256
ROPE_THETA = 500000.0
ROPE_ROT = HEAD_DIM // 4
ROPE_HALF = ROPE_ROT // 2
CHUNK = 128
GMLP_GROUPS = 4
GMLP_WIDTH = D_MODEL // 2
N_GROUPS = 4
EXPERTS_PER_GROUP = 8
N_EXPERTS = N_GROUPS * EXPERTS_PER_GROUP
EXPERT_DIM = D_MODEL // 4
PAGE_SIZE = 128
LN_EPS = 1e-5
DEEPNORM_ALPHA = 2.0 ** 0.25
MASK_BIAS = -1e30

VMEM_LIMIT_V7X = 56 * 1024 * 1024
LANES = 128

_ROW_Q = 0
_ROW_K = _ROW_Q + ATTN_WIDTH
_ROW_V = _ROW_K + ATTN_WIDTH
_ROW_QI = _ROW_V + ATTN_WIDTH
_ROW_KI = _ROW_QI + IDX_HEADS * IDX_DIM
_ROW_WI = _ROW_KI + IDX_DIM
_WI_PAD = 16
_ROWS_A = _ROW_WI + _WI_PAD
_V_ROWS = HEAD_DIM + 16
LOG2_E = 1.4426950408889634


def _cparams(sem):
    return pltpu.CompilerParams(dimension_semantics=sem, vmem_limit_bytes=VMEM_LIMIT_V7X)


def _layer_norm(x, g, b):
    mu = jnp.mean(x, axis=-1, keepdims=True)
    xc = x - mu
    var = jnp.mean(xc * xc, axis=-1, keepdims=True)
    return xc * lax.rsqrt(var + LN_EPS) * g + b


def _rope_rows(seg, cos, sin):
    x1 = seg[0:ROPE_HALF]
    x2 = seg[ROPE_HALF:ROPE_ROT]
    return jnp.concatenate([x1 * cos - x2 * sin, x2 * cos + x1 * sin, seg[ROPE_ROT:]], axis=0)


def _split_bf16(a):
    hi = a.astype(BF16)
    return hi, (a - hi.astype(F32)).astype(BF16)


def _proj_attn_kernel(x_ref, wt_ref, *rest, tk, q_scale, precise_idx):
    if precise_idx:
        wtlo_ref, rest = rest[0], rest[1:]
    (cos_ref, sin_ref,
     qz_ref, kn_ref, knb_ref, vn_ref, vtb_ref, qit_ref, kin_ref, kinb_ref, wit_ref) = rest
    x = x_ref[0]
    xb = x.astype(BF16)
    tm = xb.shape[0]
    cos = cos_ref[...]
    sin = sin_ref[...]
    nt = (((1,), (1,)), ((), ()))

    def proj_t(r0, n):
        return lax.dot_general(wt_ref[r0:r0 + n, :], xb, nt, preferred_element_type=F32)

    if precise_idx:
        xlo = (x - xb.astype(F32)).astype(BF16)

        def proj_idx(r0, n):
            w_hi = wt_ref[r0:r0 + n, :]
            return (lax.dot_general(w_hi, xb, nt, preferred_element_type=F32)
                    + lax.dot_general(w_hi, xlo, nt, preferred_element_type=F32)
                    + lax.dot_general(wtlo_ref[r0:r0 + n, :], xb, nt, preferred_element_type=F32))
    else:
        proj_idx = proj_t

    q_t = proj_t(_ROW_Q, ATTN_WIDTH)
    zeros = jnp.zeros((HEAD_DIM, tm), F32)
    for h in range(N_HEADS):
        rot = _rope_rows(q_t[h * HEAD_DIM:(h + 1) * HEAD_DIM], cos, sin) * q_scale
        pair = [rot, zeros] if h % 2 == 0 else [zeros, rot]
        qz_ref[0, h] = jnp.concatenate(pair, axis=0).astype(BF16)

    k_t = proj_t(_ROW_K, ATTN_WIDTH)
    k_rot = jnp.concatenate(
        [_rope_rows(k_t[h * HEAD_DIM:(h + 1) * HEAD_DIM], cos, sin) for h in range(N_HEADS)], axis=0)
    kn_ref[0] = k_rot
    knb_ref[0] = k_rot.T.astype(BF16)

    v_t = proj_t(_ROW_V, ATTN_WIDTH)
    vn_ref[0] = v_t
    ones = jnp.ones((_V_ROWS - HEAD_DIM, tk), F32)
    for c in range(tm // tk):
        v_c = v_t[:, c * tk:(c + 1) * tk]
        vtb_ref[0, c] = jnp.concatenate(
            [piece for h in range(N_HEADS) for piece in (v_c[h * HEAD_DIM:(h + 1) * HEAD_DIM], ones)],
            axis=0).astype(BF16)

    qi_t = proj_idx(_ROW_QI, IDX_HEADS * IDX_DIM)
    qit_ref[0] = jnp.concatenate(
        [_rope_rows(qi_t[h * IDX_DIM:(h + 1) * IDX_DIM], cos, sin) for h in range(IDX_HEADS)],
        axis=0).astype(qit_ref.dtype)

    ki_t = _rope_rows(proj_idx(_ROW_KI, IDX_DIM), cos, sin)
    kin_ref[0] = ki_t
    kinb_ref[0] = ki_t.T.astype(BF16)

    wit_ref[0] = proj_idx(_ROW_WI, _WI_PAD)[0:8]


def _proj_attn(x, wt_a, cos_t, sin_t, *, tm, tk, q_scale, wt_a_lo=None):
    b, s, d = x.shape
    grid = (b, s // tm)
    precise_idx = wt_a_lo is not None
    out_shape = (
        jax.ShapeDtypeStruct((b, N_HEADS, 2 * HEAD_DIM, s), BF16),
        jax.ShapeDtypeStruct((b, ATTN_WIDTH, s), F32),
        jax.ShapeDtypeStruct((b, s, ATTN_WIDTH), BF16),
        jax.ShapeDtypeStruct((b, ATTN_WIDTH, s), F32),
        jax.ShapeDtypeStruct((b, s // tk, N_HEADS * _V_ROWS, tk), BF16),
        jax.ShapeDtypeStruct((b, IDX_HEADS * IDX_DIM, s), F32 if precise_idx else BF16),
        jax.ShapeDtypeStruct((b, IDX_DIM, s), F32),
        jax.ShapeDtypeStruct((b, s, IDX_DIM), BF16),
        jax.ShapeDtypeStruct((b, 8, s), F32),
    )
    out_specs = (
        pl.BlockSpec((1, N_HEADS, 2 * HEAD_DIM, tm), lambda bi, i: (bi, 0, 0, i)),
        pl.BlockSpec((1, ATTN_WIDTH, tm), lambda bi, i: (bi, 0, i)),
        pl.BlockSpec((1, tm, ATTN_WIDTH), lambda bi, i: (bi, i, 0)),
        pl.BlockSpec((1, ATTN_WIDTH, tm), lambda bi, i: (bi, 0, i)),
        pl.BlockSpec((1, tm // tk, N_HEADS * _V_ROWS, tk), lambda bi, i: (bi, i, 0, 0)),
        pl.BlockSpec((1, IDX_HEADS * IDX_DIM, tm), lambda bi, i: (bi, 0, i)),
        pl.BlockSpec((1, IDX_DIM, tm), lambda bi, i: (bi, 0, i)),
        pl.BlockSpec((1, tm, IDX_DIM), lambda bi, i: (bi, i, 0)),
        pl.BlockSpec((1, 8, tm), lambda bi, i: (bi, 0, i)),
    )
    weights = [wt_a] + ([wt_a_lo] if precise_idx else [])
    in_specs = (
        [pl.BlockSpec((1, tm, d), lambda bi, i: (bi, i, 0))]
        + [pl.BlockSpec((_ROWS_A, d), lambda bi, i: (0, 0)) for _ in weights]
        + [pl.BlockSpec((ROPE_HALF, tm), lambda bi, i: (0, i)),
           pl.BlockSpec((ROPE_HALF, tm), lambda bi, i: (0, i))])
    return pl.pallas_call(
        functools.partial(_proj_attn_kernel, tk=tk, q_scale=q_scale, precise_idx=precise_idx),
        grid=grid, in_specs=in_specs, out_specs=out_specs, out_shape=out_shape,
        compiler_params=_cparams(("parallel", "parallel")), name="proj_attn",
    )(x, *weights, cos_t, sin_t)


def _gmlp_kernel(x_ref, w_ref, g_ref, b_ref, ws_ref, bst_ref, gm_ref, vn_ref, *, single_row_chunks):
    xb = x_ref[0].astype(BF16)
    tm = xb.shape[0]
    u = jnp.dot(xb, w_ref[:, :GMLP_WIDTH], preferred_element_type=F32)
    vb = jnp.dot(xb, w_ref[:, GMLP_WIDTH:], preferred_element_type=F32)
    vn = _layer_norm(vb, g_ref[...], b_ref[...])
    vn_ref[0] = vn
    gw = GMLP_WIDTH // GMLP_GROUPS
    row = lax.broadcasted_iota(jnp.int32, (CHUNK, CHUNK), 0)
    col = lax.broadcasted_iota(jnp.int32, (CHUNK, CHUNK), 1)
    if single_row_chunks:
        for g in range(GMLP_GROUPS):
            w00 = ws_ref[g][0:1, 0:1].astype(BF16).astype(F32)
            vg = vn[:, g * gw:(g + 1) * gw].astype(BF16).astype(F32)
            mix = vg * w00 + bst_ref[0:1, g:g + 1]
            gm_ref[0, :, g * gw:(g + 1) * gw] = (u[:, g * gw:(g + 1) * gw] * mix).astype(BF16)
    else:
        vnb = vn.astype(BF16)
        for g in range(GMLP_GROUPS):
            wsg = jnp.where(row >= col, ws_ref[g], 0.0).astype(BF16)
            bias = bst_ref[:, g:g + 1]
            for c in range(tm // CHUNK):
                mix = jnp.dot(wsg, vnb[c * CHUNK:(c + 1) * CHUNK, g * gw:(g + 1) * gw],
                              preferred_element_type=F32) + bias
                gm_ref[0, c * CHUNK:(c + 1) * CHUNK, g * gw:(g + 1) * gw] = (
                    u[c * CHUNK:(c + 1) * CHUNK, g * gw:(g + 1) * gw] * mix).astype(BF16)


def _gmlp(x, w_uv, ln_g, ln_b, w_spatial, b_spatial_t, *, tm, single_row_chunks):
    b, s, d = x.shape
    return pl.pallas_call(
        functools.partial(_gmlp_kernel, single_row_chunks=single_row_chunks),
        grid=(b, s // tm),
        in_specs=[
            pl.BlockSpec((1, tm, d), lambda bi, i: (bi, i, 0)),
            pl.BlockSpec((d, 2 * GMLP_WIDTH), lambda bi, i: (0, 0)),
            pl.BlockSpec((1, GMLP_WIDTH), lambda bi, i: (0, 0)),
            pl.BlockSpec((1, GMLP_WIDTH), lambda bi, i: (0, 0)),
            pl.BlockSpec((GMLP_GROUPS, CHUNK, CHUNK), lambda bi, i: (0, 0, 0)),
            pl.BlockSpec((CHUNK, GMLP_GROUPS), lambda bi, i: (0, 0)),
        ],
        out_specs=(pl.BlockSpec((1, tm, GMLP_WIDTH), lambda bi, i: (bi, i, 0)),
                   pl.BlockSpec((1, tm, GMLP_WIDTH), lambda bi, i: (bi, i, 0))),
        out_shape=(jax.ShapeDtypeStruct((b, s, GMLP_WIDTH), BF16),
                   jax.ShapeDtypeStruct((b, s, GMLP_WIDTH), F32)),
        compiler_params=_cparams(("parallel", "parallel")), name="gmlp",
    )(x, w_uv, ln_g, ln_b, w_spatial, b_spatial_t)


def _select_threshold(count_ge, count_gt, max_below, smin, smax, n_valid):
    k = float(TOPK)
    one = jnp.ones_like(smin)
    zero = jnp.zeros_like(smin)
    flag = lambda cnd: jnp.where(cnd, one, zero)
    live = flag(n_valid > k)
    lo, c_lo = smin, n_valid
    hi, c_hi = smax + (jnp.abs(smax) * 1e-6 + 1e-30), zero
    c0 = count_ge(zero)
    c0p = count_gt(zero)
    ge0 = flag(c0 >= k)
    raise_lo = live * ge0 * flag(zero > lo)
    lower_hi = live * (1.0 - ge0) * flag(zero < hi)
    lo, c_lo = jnp.where(raise_lo > 0.0, zero, lo), jnp.where(raise_lo > 0.0, c0, c_lo)
    hi, c_hi = jnp.where(lower_hi > 0.0, zero, hi), jnp.where(lower_hi > 0.0, c0, c_hi)
    live = live * (1.0 - ge0 * flag(c0p < k))
    log_k = float(np.log(k))

    def cond(st):
        _, _, c_lo, _, c_hi, live = st
        return jnp.sum(live * flag(c_lo - c_hi > 2.0)) > 0.0

    def body(st):
        it, lo, c_lo, hi, c_hi, live = st
        half = 0.5 * lo + 0.5 * hi
        log_lo = jnp.log(c_lo)
        frac = (log_lo - log_k) / jnp.maximum(log_lo - jnp.log(jnp.maximum(c_hi, 0.5)), 1e-6)
        interp = lo + (hi - lo) * jnp.clip(frac, 0.02, 0.98)
        inside = jnp.logical_and(interp > lo, interp < hi)
        mid = jnp.where(jnp.logical_and(it % 2 == 0, inside), interp, half)
        upd = live * flag(jnp.logical_and(half > lo, half < hi))
        c = count_ge(mid)
        ge = flag(c >= k)
        up, down = upd * ge > 0.0, upd * (1.0 - ge) > 0.0
        lo, c_lo = jnp.where(up, mid, lo), jnp.where(up, c, c_lo)
        hi, c_hi = jnp.where(down, mid, hi), jnp.where(down, c, c_hi)
        live = upd * flag(c != k)
        return it + 1, lo, c_lo, hi, c_hi, live

    _, lo, _, hi, _, live = lax.while_loop(cond, body, (jnp.int32(0), lo, c_lo, hi, c_hi, live))
    thr = jnp.where(live > 0.0, max_below(hi), lo)
    m = k - count_gt(thr)
    return thr, m


def _prompt_attn_kernel(qz_ref, knb_ref, vtb_ref, qit_ref, kinb_ref, wit_ref, out_ref,
                        sc_ref, acc_ref, *, tq):
    tk = tq
    i = pl.program_id(1)
    nkc = i + 1
    wi = wit_ref[0]
    inf = jnp.float32(jnp.inf)

    def raw_scores(c):
        r0 = pl.multiple_of(c * tk, tk)
        kic = kinb_ref[0, pl.ds(r0, tk), :]
        ds = [jnp.dot(kic, qit_ref[0, h * IDX_DIM:(h + 1) * IDX_DIM, :], preferred_element_type=F32)
              for h in range(IDX_HEADS)]
        s = jnp.maximum(ds[0], 0.0) * wi[0:1, :]
        for h in range(1, IDX_HEADS):
            s = s + jnp.maximum(ds[h], 0.0) * wi[h:h + 1, :]
        return r0, s

    def score_chunk(c, carry):
        smin, smax = carry
        r0, s = raw_scores(c)
        sc_ref[pl.ds(r0, tk), :] = s
        return (jnp.minimum(smin, jnp.min(s, axis=0, keepdims=True)),
                jnp.maximum(smax, jnp.max(s, axis=0, keepdims=True)))

    smin, smax = lax.fori_loop(0, i, score_chunk,
                               (jnp.full((1, tq), inf, F32), jnp.full((1, tq), -inf, F32)))
    r0, s = raw_scores(i)
    vis = lax.broadcasted_iota(jnp.int32, (tk, tq), 0) <= lax.broadcasted_iota(jnp.int32, (tk, tq), 1)
    s_vis = jnp.where(vis, s, -inf)
    sc_ref[pl.ds(r0, tk), :] = s_vis
    smax = jnp.maximum(smax, jnp.max(s_vis, axis=0, keepdims=True))
    smin = jnp.minimum(smin, jnp.min(jnp.where(vis, s, inf), axis=0, keepdims=True))

    @pl.when(nkc % 2 == 1)
    def _():
        sc_ref[pl.ds(pl.multiple_of(nkc * tk, tk), tk), :] = jnp.full((tk, tq), -inf, F32)

    def make_count(strict):
        def count(v):
            def body(c, acc):
                for half in range(2):
                    r0 = pl.multiple_of((2 * c + half) * tk, tk)
                    blk = sc_ref[pl.ds(r0, tk), :]
                    hit = (blk > v) if strict else (blk >= v)
                    acc = acc + jnp.sum(jnp.where(hit, 1.0, 0.0).reshape(tk // 8, 8, tq), axis=0)
                return acc
            acc = lax.fori_loop(0, (nkc + 1) // 2, body, jnp.zeros((8, tq), F32))
            return jnp.sum(acc, axis=0, keepdims=True)
        return count

    def max_below(v):
        def body(c, acc):
            blk = sc_ref[pl.ds(pl.multiple_of(c * tk, tk), tk), :]
            return jnp.maximum(acc, jnp.max(jnp.where(blk < v, blk, -inf).reshape(tk // 8, 8, tq), axis=0))
        acc = lax.fori_loop(0, nkc, body, jnp.full((8, tq), -inf, F32))
        return jnp.max(acc, axis=0, keepdims=True)

    n_valid = (i * tq + lax.broadcasted_iota(jnp.int32, (1, tq), 1) + 1).astype(F32)
    thr, m_take = _select_threshold(make_count(False), make_count(True), max_below, smin, smax, n_valid)

    acc_ref[...] = jnp.zeros_like(acc_ref)
    tri = (lax.broadcasted_iota(jnp.int32, (tk, tk), 0) >=
           lax.broadcasted_iota(jnp.int32, (tk, tk), 1)).astype(BF16)

    def attend_chunk(c, carry):
        tie_seen, ms, ls = carry
        r0 = pl.multiple_of(c * tk, tk)
        blk = sc_ref[pl.ds(r0, tk), :]
        tie = jnp.where(blk == thr, 1.0, 0.0).astype(BF16)
        rank = jnp.dot(tri, tie, preferred_element_type=F32) + tie_seen
        tie_rank = jnp.where(blk == thr, rank, inf)
        bias = jnp.where(blk > thr, 0.0, jnp.where(tie_rank <= m_take, 0.0, MASK_BIAS))
        kc = knb_ref[0, pl.ds(r0, tk), :]
        lgs = []
        for h in range(N_HEADS):
            k2 = kc[:, (h // 2) * 2 * HEAD_DIM:(h // 2 + 1) * 2 * HEAD_DIM]
            lgs.append(jnp.dot(k2, qz_ref[0, h], preferred_element_type=F32) + bias)
        new_ms = [jnp.maximum(ms[h], jnp.max(lgs[h], axis=0, keepdims=True)) for h in range(N_HEADS)]
        ps = [jnp.exp2(lgs[h] - new_ms[h]).astype(BF16) for h in range(N_HEADS)]
        pvs = [jnp.dot(vtb_ref[0, c, h * _V_ROWS:(h + 1) * _V_ROWS, :], ps[h], preferred_element_type=F32)
               for h in range(N_HEADS)]
        new_ls = []
        for h in range(N_HEADS):
            alpha = jnp.exp2(ms[h] - new_ms[h])
            acc_ref[h * HEAD_DIM:(h + 1) * HEAD_DIM, :] = (
                acc_ref[h * HEAD_DIM:(h + 1) * HEAD_DIM, :] * alpha + pvs[h][0:HEAD_DIM])
            new_ls.append(ls[h] * alpha + pvs[h][HEAD_DIM:HEAD_DIM + 1])
        return rank[tk - 1:tk, :], tuple(new_ms), tuple(new_ls)

    init = (jnp.zeros((1, tq), F32),
            tuple(jnp.full((1, tq), MASK_BIAS, F32) for _ in range(N_HEADS)),
            tuple(jnp.zeros((1, tq), F32) for _ in range(N_HEADS)))
    _, _, ls = lax.fori_loop(0, nkc, attend_chunk, init)

    out_t = jnp.concatenate(
        [acc_ref[h * HEAD_DIM:(h + 1) * HEAD_DIM, :] / ls[h] for h in range(N_HEADS)], axis=0)
    out_ref[0] = out_t.T.astype(BF16)


def _prompt_attn(qz, knb, vtb, qit, kinb, wit, *, tq):
    b, s, _ = knb.shape
    return pl.pallas_call(
        functools.partial(_prompt_attn_kernel, tq=tq),
        grid=(b, s // tq),
        in_specs=[
            pl.BlockSpec((1, N_HEADS, 2 * HEAD_DIM, tq), lambda bi, i: (bi, 0, 0, i)),
            pl.BlockSpec((1, s, ATTN_WIDTH), lambda bi, i: (bi, 0, 0), pipeline_mode=pl.Buffered(1)),
            pl.BlockSpec((1, s // tq, N_HEADS * _V_ROWS, tq), lambda bi, i: (bi, 0, 0, 0),
                         pipeline_mode=pl.Buffered(1)),
            pl.BlockSpec((1, IDX_HEADS * IDX_DIM, tq), lambda bi, i: (bi, 0, i)),
            pl.BlockSpec((1, s, IDX_DIM), lambda bi, i: (bi, 0, 0), pipeline_mode=pl.Buffered(1)),
            pl.BlockSpec((1, 8, tq), lambda bi, i: (bi, 0, i)),
        ],
        out_specs=pl.BlockSpec((1, tq, ATTN_WIDTH), lambda bi, i: (bi, i, 0)),
        out_shape=jax.ShapeDtypeStruct((b, s, ATTN_WIDTH), BF16),
        scratch_shapes=[
            pltpu.VMEM((s + tq, tq), F32),
            pltpu.VMEM((ATTN_WIDTH, tq), F32),
        ],
        compiler_params=_cparams(("parallel", "arbitrary")), name="prompt_attn",
    )(qz, knb, vtb, qit, kinb, wit)


_GROUP_LANE = N_EXPERTS + N_GROUPS


def _route(rl):
    lane = lax.broadcasted_iota(jnp.int32, rl.shape, 1).astype(F32)
    neg = jnp.float32(-jnp.inf)
    big = jnp.float32(1e9)
    g0 = float(N_EXPERTS)
    lg = jnp.where(jnp.logical_and(lane >= g0, lane < g0 + N_GROUPS), rl, neg)
    gmax = jnp.max(lg, axis=1, keepdims=True)
    gsel = jnp.min(jnp.where(lg == gmax, lane, big), axis=1, keepdims=True) - g0
    pgsel = 1.0 / jnp.sum(jnp.exp(lg - gmax), axis=1, keepdims=True)
    e0 = gsel * EXPERTS_PER_GROUP
    le = jnp.where(jnp.logical_and(lane >= e0, lane < e0 + EXPERTS_PER_GROUP), rl, neg)
    v1 = jnp.max(le, axis=1, keepdims=True)
    i1 = jnp.min(jnp.where(le == v1, lane, big), axis=1, keepdims=True)
    le2 = jnp.where(lane == i1, neg, le)
    v2 = jnp.max(le2, axis=1, keepdims=True)
    i2 = jnp.min(jnp.where(le2 == v2, lane, big), axis=1, keepdims=True)
    ex = jnp.exp(v2 - v1)
    den = 1.0 + ex
    p1 = pgsel / den
    p2 = pgsel * ex / den
    comb = jnp.where(lane == i1, p1, jnp.where(lane == i2, p2, 0.0))
    return jnp.where(lane == float(_GROUP_LANE), gsel, comb)


def _merge_kernel(x_ref, attn_ref, gm_ref, wgate_ref, gbias_ref, wa_ref, wb_ref, wo_ref,
                  lng_ref, lnb_ref, wr_ref, br_ref, hr_ref, *, precise_router):
    x = x_ref[0]
    xb = x.astype(BF16)
    ga = jax.nn.sigmoid(jnp.dot(xb, wgate_ref[:, :D_MODEL], preferred_element_type=F32) + gbias_ref[0:1, :])
    gb = jax.nn.sigmoid(jnp.dot(xb, wgate_ref[:, D_MODEL:], preferred_element_type=F32) + gbias_ref[1:2, :])
    a = jnp.dot(attn_ref[0], wa_ref[...], preferred_element_type=F32)
    bm = jnp.dot(gm_ref[0], wb_ref[...], preferred_element_type=F32)
    merged = ga * a + gb * bm
    y1 = DEEPNORM_ALPHA * x + jnp.dot(merged.astype(BF16), wo_ref[...], preferred_element_type=F32)
    h = _layer_norm(y1, lng_ref[...], lnb_ref[...])
    hr_ref[0, :, 0:D_MODEL] = h
    if precise_router:
        h_hi, h_lo = _split_bf16(h)
        w_hi, w_lo = _split_bf16(wr_ref[...])
        rl = (jnp.dot(h_hi, w_hi, preferred_element_type=F32) + jnp.dot(h_lo, w_hi, preferred_element_type=F32)
              + jnp.dot(h_hi, w_lo, preferred_element_type=F32))
    else:
        rl = jnp.dot(h.astype(BF16), wr_ref[...].astype(BF16), preferred_element_type=F32)
    hr_ref[0, :, D_MODEL:] = _route(rl + br_ref[...])


def _merge(x, attn, gm, p, *, tm, precise_router):
    b, s, d = x.shape
    const = lambda shape: pl.BlockSpec(shape, lambda bi, i: tuple(0 for _ in shape))
    return pl.pallas_call(
        functools.partial(_merge_kernel, precise_router=precise_router),
        grid=(b, s // tm),
        in_specs=[
            pl.BlockSpec((1, tm, d), lambda bi, i: (bi, i, 0)),
            pl.BlockSpec((1, tm, ATTN_WIDTH), lambda bi, i: (bi, i, 0)),
            pl.BlockSpec((1, tm, GMLP_WIDTH), lambda bi, i: (bi, i, 0)),
            const((d, 2 * d)), const((2, d)), const((ATTN_WIDTH, d)), const((GMLP_WIDTH, d)), const((d, d)),
            const((1, d)), const((1, d)), const((d, LANES)), const((1, LANES)),
        ],
        out_specs=pl.BlockSpec((1, tm, d + LANES), lambda bi, i: (bi, i, 0)),
        out_shape=jax.ShapeDtypeStruct((b, s, d + LANES), F32),
        compiler_params=_cparams(("parallel", "parallel")), name="merge_ln1_route",
    )(x, attn, gm, p["w_gates"], p["gate_bias"], p["w_branch_a"], p["w_branch_b"], p["w_o"],
      p["ln1_g"], p["ln1_b"], p["w_router"], p["b_router"])


def _moe_kernel(h_ref, comb_ref, wg_ref, wu_ref, wd_ref, lng_ref, lnb_ref, y_ref, acc_ref, hb_ref):
    e = pl.program_id(1)

    @pl.when(e == 0)
    def _():
        acc_ref[...] = jnp.zeros_like(acc_ref)
        hb_ref[...] = h_ref[...].astype(BF16)

    comb = comb_ref[...]
    lane = lax.broadcasted_iota(jnp.int32, comb.shape, 1)
    wcol = jnp.sum(jnp.where(lane == e, comb, 0.0), axis=1, keepdims=True)
    tm = hb_ref.shape[0]
    halves = [slice(r, r + tm // 2) for r in (0, tm // 2)] if tm % 16 == 0 else [slice(0, tm)]
    gates = [jnp.dot(hb_ref[rows, :], wg_ref[0], preferred_element_type=F32) for rows in halves]
    ups = [jnp.dot(hb_ref[rows, :], wu_ref[0], preferred_element_type=F32) for rows in halves]
    for rows, gate, up in zip(halves, gates, ups):
        hg = jax.nn.silu(gate) * up * wcol[rows]
        acc_ref[rows, :] += jnp.dot(hg.astype(BF16), wd_ref[0], preferred_element_type=F32)

    @pl.when(e == pl.num_programs(1) - 1)
    def _():
        y_ref[...] = _layer_norm(DEEPNORM_ALPHA * h_ref[...] + acc_ref[...], lng_ref[...], lnb_ref[...])


def _moe(hr, p, *, tm):
    t, d = hr.shape[0], hr.shape[1] - LANES
    return pl.pallas_call(
        _moe_kernel,
        grid=(t // tm, N_EXPERTS),
        in_specs=[
            pl.BlockSpec((tm, d), lambda i, e: (i, 0)),
            pl.BlockSpec((tm, LANES), lambda i, e: (i, d // LANES)),
            pl.BlockSpec((1, d, EXPERT_DIM), lambda i, e: (e, 0, 0)),
            pl.BlockSpec((1, d, EXPERT_DIM), lambda i, e: (e, 0, 0)),
            pl.BlockSpec((1, EXPERT_DIM, d), lambda i, e: (e, 0, 0)),
            pl.BlockSpec((1, d), lambda i, e: (0, 0)),
            pl.BlockSpec((1, d), lambda i, e: (0, 0)),
        ],
        out_specs=pl.BlockSpec((tm, d), lambda i, e: (i, 0)),
        out_shape=jax.ShapeDtypeStruct((t, d), F32),
        scratch_shapes=[pltpu.VMEM((tm, d), F32), pltpu.VMEM((tm, d), BF16)],
        compiler_params=_cparams(("parallel", "arbitrary")), name="moe_ln2",
    )(hr, hr, p["w_expert_gate"], p["w_expert_up"], p["w_expert_down"], p["ln2_g"], p["ln2_b"])


_MOE_TILE = 512


def _moe_sorted_kernel(tg_ref, nreal_ref, src_ref, hr_hbm, wg_ref, wu_ref, wd_ref, lng_ref, lnb_ref, y_hbm,
                       hbuf, hb_ref, acc_ref, ybuf, gsem, ssem):
    i = pl.program_id(0)
    e = pl.program_id(1)
    n_tiles = pl.num_programs(0)
    tm = hb_ref.shape[0]
    d = hb_ref.shape[1]
    slot = i % 2

    def start_gather(tile, slot_):
        def body(r, carry):
            pltpu.make_async_copy(hr_hbm.at[src_ref[tile * tm + r]], hbuf.at[slot_, r], gsem.at[slot_]).start()
            return carry
        lax.fori_loop(0, tm, body, 0, unroll=8)

    def wait_gather(slot_):
        pltpu.make_async_copy(hbuf.at[slot_], hbuf.at[slot_], gsem.at[slot_]).wait()

    def start_write_back(tile, slot_):
        def start_row(r):
            pltpu.make_async_copy(ybuf.at[slot_, r], y_hbm.at[src_ref[tile * tm + r]], ssem.at[slot_]).start()

        def body8(j, carry):
            for u in range(8):
                start_row(j * 8 + u)
            return carry

        def body1(r, carry):
            start_row(r)
            return carry
        n = nreal_ref[tile]
        lax.fori_loop(0, n // 8, body8, 0)
        lax.fori_loop((n // 8) * 8, n, body1, 0)

    def wait_write_back(tile, slot_):
        n = nreal_ref[tile]
        n_tiled = pl.multiple_of((n // 8) * 8, 8)

        @pl.when(n_tiled > 0)
        def _():
            rows = ybuf.at[slot_, pl.ds(0, n_tiled)]
            pltpu.make_async_copy(rows, rows, ssem.at[slot_]).wait()

        def body(r, carry):
            pltpu.make_async_copy(ybuf.at[slot_, r], ybuf.at[slot_, r], ssem.at[slot_]).wait()
            return carry
        lax.fori_loop(n_tiled, n, body, 0)

    @pl.when(e == 0)
    def _():
        @pl.when(i == 0)
        def _():
            start_gather(0, 0)
        wait_gather(slot)
        hb_ref[...] = hbuf[slot, :, 0:d].astype(BF16)
        acc_ref[...] = jnp.zeros_like(acc_ref)

    nxt = jnp.minimum(i + 1, n_tiles - 1)
    share = tm // EXPERTS_PER_GROUP
    for u in range(share):
        r = e * share + u
        pltpu.make_async_copy(hr_hbm.at[src_ref[nxt * tm + r]], hbuf.at[1 - slot, r], gsem.at[1 - slot]).start()

    comb = hbuf[slot, :, d:]
    lane = lax.broadcasted_iota(jnp.int32, comb.shape, 1)
    expert = tg_ref[i] * EXPERTS_PER_GROUP + e
    wcol = jnp.sum(jnp.where(lane == expert, comb, 0.0), axis=1, keepdims=True)
    halves = [slice(r, r + tm // 2) for r in (0, tm // 2)]
    gates = [jnp.dot(hb_ref[rows, :], wg_ref[0], preferred_element_type=F32) for rows in halves]
    ups = [jnp.dot(hb_ref[rows, :], wu_ref[0], preferred_element_type=F32) for rows in halves]
    for rows, gate, up in zip(halves, gates, ups):
        hg = jax.nn.silu(gate) * up * wcol[rows]
        acc_ref[rows, :] += jnp.dot(hg.astype(BF16), wd_ref[0], preferred_element_type=F32)

    @pl.when(e == pl.num_programs(1) - 1)
    def _():
        @pl.when(i >= 2)
        def _():
            wait_write_back(i - 2, slot)
        ybuf[slot] = _layer_norm(DEEPNORM_ALPHA * hbuf[slot, :, 0:d] + acc_ref[...], lng_ref[...], lnb_ref[...])
        start_write_back(i, slot)

        @pl.when(i == n_tiles - 1)
        def _():
            wait_gather(1 - slot)

            @pl.when(i >= 1)
            def _():
                wait_write_back(i - 1, 1 - slot)
            wait_write_back(i, slot)


def _moe_sorted(hr, p):
    t, d = hr.shape[0], hr.shape[1] - LANES
    tm = _MOE_TILE
    assert t % tm == 0
    grp = hr[:, d + _GROUP_LANE].astype(jnp.int32)
    counts = jnp.sum(grp[:, None] == jnp.arange(N_GROUPS, dtype=jnp.int32)[None, :], axis=0).astype(jnp.int32)
    tiles = (counts + tm - 1) // tm
    tile_end = jnp.cumsum(tiles)
    tile_start = tile_end - tiles
    n_tiles = t // tm + N_GROUPS
    tile_id = jnp.arange(n_tiles, dtype=jnp.int32)
    tile_group = jnp.minimum(jnp.sum(tile_id[:, None] >= tile_end[None, :], axis=1),
                             N_GROUPS - 1).astype(jnp.int32)
    n_pad = n_tiles * tm - t
    pad_end = jnp.cumsum(tiles * tm - counts)
    pad_group = jnp.sum(jnp.arange(n_pad, dtype=jnp.int32)[:, None] >= pad_end[None, :], axis=1).astype(jnp.int32)
    keys = jnp.concatenate([2 * grp, 2 * pad_group + 1])
    toks = jnp.concatenate([jnp.arange(t, dtype=jnp.int32), jnp.zeros((n_pad,), jnp.int32)])
    _, src = lax.sort_key_val(keys, toks, is_stable=True)
    n_real = jnp.clip(counts[tile_group] - (tile_id - tile_start[tile_group]) * tm, 0, tm).astype(jnp.int32)
    n_real = jnp.where(tile_id < tile_end[-1], n_real, 0)

    expert_block = lambda shape: pl.BlockSpec(
        shape, lambda i, e, tg, nr, sr: (tg[i] * EXPERTS_PER_GROUP + e, 0, 0))
    const_row = pl.BlockSpec((1, d), lambda i, e, tg, nr, sr: (0, 0))
    return pl.pallas_call(
        _moe_sorted_kernel,
        grid_spec=pltpu.PrefetchScalarGridSpec(
            num_scalar_prefetch=3, grid=(n_tiles, EXPERTS_PER_GROUP),
            in_specs=[pl.BlockSpec(memory_space=pl.ANY),
                      expert_block((1, d, EXPERT_DIM)), expert_block((1, d, EXPERT_DIM)),
                      expert_block((1, EXPERT_DIM, d)), const_row, const_row],
            out_specs=pl.BlockSpec(memory_space=pl.ANY),
            scratch_shapes=[pltpu.VMEM((2, tm, d + LANES), F32),
                            pltpu.VMEM((tm, d), BF16), pltpu.VMEM((tm, d), F32), pltpu.VMEM((2, tm, d), F32),
                            pltpu.SemaphoreType.DMA((2,)), pltpu.SemaphoreType.DMA((2,))],
        ),
        out_shape=jax.ShapeDtypeStruct((t, d), F32),
        compiler_params=_cparams(("arbitrary", "arbitrary")), name="moe_sorted_ln2",
    )(tile_group, n_real, src.astype(jnp.int32), hr, p["w_expert_gate"], p["w_expert_up"], p["w_expert_down"],
      p["ln2_g"], p["ln2_b"])


_PAGES_PER_STEP = 8
_SCORE_PAGES_PER_STEP = 32


def _page_specs(block, n):
    zeros = (0,) * (len(block) - 1)

    def spec(r):
        return pl.BlockSpec(block, lambda b, j, pt: (pt[b, j * n + r],) + zeros)
    return [spec(r) for r in range(n)]


def _sample_scores_kernel(pt_ref, qi_ref, wi_ref, *refs):
    page_refs, out_ref = refs[:-1], refs[-1]
    q_hi, q_lo = _split_bf16(qi_ref[0])
    q2 = jnp.concatenate([q_hi, q_lo], axis=0)
    wi = wi_ref[0][:, 0:1]
    pages = page_refs[0][0] if len(page_refs) == 1 else jnp.concatenate([p[0] for p in page_refs], axis=1)
    k_hi, k_lo = _split_bf16(pages)
    d2 = jnp.dot(q2, k_hi, preferred_element_type=F32)
    d = d2[0:8] + d2[8:16] + jnp.dot(q_hi, k_lo, preferred_element_type=F32)
    s = jnp.sum(jnp.maximum(d, 0.0) * wi, axis=0, keepdims=True)
    for r in range(len(page_refs)):
        out_ref[0, r:r + 1, :] = s[:, r * PAGE_SIZE:(r + 1) * PAGE_SIZE]


def _sample_scores(page_table, qi_s, wi_s, kidx_t, *, n):
    db, n_pages = page_table.shape
    return pl.pallas_call(
        _sample_scores_kernel,
        grid_spec=pltpu.PrefetchScalarGridSpec(
            num_scalar_prefetch=1, grid=(db, n_pages // n),
            in_specs=[pl.BlockSpec((1, 8, IDX_DIM), lambda b, j, pt: (b, 0, 0)),
                      pl.BlockSpec((1, 8, LANES), lambda b, j, pt: (b, 0, 0))]
            + _page_specs((1, IDX_DIM, PAGE_SIZE), n),
            out_specs=pl.BlockSpec((1, n, PAGE_SIZE), lambda b, j, pt: (b, j, 0)),
        ),
        out_shape=jax.ShapeDtypeStruct((db, n_pages, PAGE_SIZE), F32),
        compiler_params=_cparams(("parallel", "arbitrary")), name="sample_scores",
    )(page_table, qi_s, wi_s, *([kidx_t] * n))


def _sample_select_kernel(sc_ref, nvalid_ref, sel_ref, *, tk):
    nk, tq = sc_ref.shape
    nkc = nk // tk

    def chunk_reduce(fn, init):
        def body(c, acc):
            r0 = pl.multiple_of(c * tk, tk)
            return fn(acc, sc_ref[pl.ds(r0, tk), :])
        return lax.fori_loop(0, nkc, body, init)

    inf = jnp.float32(jnp.inf)
    smax = chunk_reduce(lambda a, blk: jnp.maximum(a, jnp.max(blk, axis=0, keepdims=True)),
                        jnp.full((1, tq), -inf, F32))
    smin = chunk_reduce(
        lambda a, blk: jnp.minimum(a, jnp.min(jnp.where(blk == -inf, inf, blk), axis=0, keepdims=True)),
        jnp.full((1, tq), inf, F32))

    def make_count(strict):
        def count(v):
            def fn(acc, blk):
                ind = jnp.where((blk > v) if strict else (blk >= v), 1.0, 0.0)
                return acc + jnp.sum(ind.reshape(tk // 8, 8, tq), axis=0)
            return jnp.sum(chunk_reduce(fn, jnp.zeros((8, tq), F32)), axis=0, keepdims=True)
        return count

    def max_below(v):
        fn = lambda acc, blk: jnp.maximum(
            acc, jnp.max(jnp.where(blk < v, blk, -inf).reshape(tk // 8, 8, tq), axis=0))
        return jnp.max(chunk_reduce(fn, jnp.full((8, tq), -inf, F32)), axis=0, keepdims=True)

    thr, m_take = _select_threshold(make_count(False), make_count(True), max_below, smin, smax,
                                    nvalid_ref[...])
    tri = (lax.broadcasted_iota(jnp.int32, (tk, tk), 0) >=
           lax.broadcasted_iota(jnp.int32, (tk, tk), 1)).astype(BF16)

    def emit(c, tie_seen):
        r0 = pl.multiple_of(c * tk, tk)
        blk = sc_ref[pl.ds(r0, tk), :]
        tie = jnp.where(blk == thr, 1.0, 0.0)
        rank = jnp.dot(tri, tie.astype(BF16), preferred_element_type=F32) + tie_seen
        keep_tie = tie * jnp.where(rank <= m_take, 1.0, 0.0)
        sel_ref[pl.ds(r0, tk), :] = jnp.where(blk > thr, 1.0, keep_tie)
        return tie_seen + jnp.sum(tie, axis=0, keepdims=True)

    lax.fori_loop(0, nkc, emit, jnp.zeros((1, tq), F32))


def _sample_select(sc_t, n_valid, *, tk):
    nk, tq = sc_t.shape
    return pl.pallas_call(
        functools.partial(_sample_select_kernel, tk=tk),
        grid=(1,),
        in_specs=[pl.BlockSpec((nk, tq), lambda i: (0, 0)), pl.BlockSpec((1, tq), lambda i: (0, 0))],
        out_specs=pl.BlockSpec((nk, tq), lambda i: (0, 0)),
        out_shape=jax.ShapeDtypeStruct((nk, tq), F32),
        compiler_params=_cparams(("arbitrary",)), name="sample_select",
    )(sc_t, n_valid)


def _sample_attend_kernel(pt_ref, qb_ref, sel_ref, knew_ref, vnew_ref, selnew_ref, *refs, n):
    k_refs, v_refs = refs[:n], refs[n:2 * n]
    out_ref, acc_ref, m_ref, l_ref = refs[2 * n:]
    j = pl.program_id(1)
    lane = lax.broadcasted_iota(jnp.int32, (1, PAGE_SIZE), 1)

    def update(h, lgs, vals):
        m_old = m_ref[h:h + 1, :]
        m_new = m_old
        for lg in lgs:
            m_new = jnp.maximum(m_new, jnp.max(lg, axis=1, keepdims=True))
        alpha = jnp.exp(m_old - m_new)
        l_new = alpha * l_ref[h:h + 1, :]
        acc = acc_ref[h] * alpha
        for lg, val in zip(lgs, vals):
            p = jnp.exp(lg - m_new)
            l_new = l_new + jnp.sum(p, axis=1, keepdims=True)
            acc = acc + val * p
        m_ref[h:h + 1, :] = m_new
        l_ref[h:h + 1, :] = l_new
        acc_ref[h] = acc

    @pl.when(j == 0)
    def _():
        m_ref[...] = jnp.full_like(m_ref, MASK_BIAS)
        l_ref[...] = jnp.zeros_like(l_ref)
        acc_ref[...] = jnp.zeros_like(acc_ref)
        for h in range(N_HEADS):
            lg = jnp.sum(knew_ref[0, h] * qb_ref[0, h], axis=0, keepdims=True)
            keep = jnp.logical_and(lane == 0, selnew_ref[0, h:h + 1, :] > 0.0)
            update(h, [jnp.where(keep, lg, MASK_BIAS)], [vnew_ref[0, h]])

    biases = [jnp.where(sel_ref[0, r:r + 1, :] > 0.0, 0.0, MASK_BIAS) for r in range(n)]
    for h in range(N_HEADS):
        q_h = qb_ref[0, h]
        lgs = [jnp.sum(k_refs[r][0, h] * q_h, axis=0, keepdims=True) + biases[r] for r in range(n)]
        update(h, lgs, [v_refs[r][0, h] for r in range(n)])

    @pl.when(j == pl.num_programs(1) - 1)
    def _():
        for h in range(N_HEADS):
            out_ref[0, h] = jnp.sum(acc_ref[h], axis=1, keepdims=True) / l_ref[h:h + 1, 0:1]


def _sample_attend(page_table, qb, sel, k_new, v_new, sel_new, ck_t, cv_t):
    db, n_pages = page_table.shape
    n = _PAGES_PER_STEP
    head_tile = (1, N_HEADS, HEAD_DIM, PAGE_SIZE)
    per_sample = pl.BlockSpec(head_tile, lambda b, j, pt: (b, 0, 0, 0))
    return pl.pallas_call(
        functools.partial(_sample_attend_kernel, n=n),
        grid_spec=pltpu.PrefetchScalarGridSpec(
            num_scalar_prefetch=1, grid=(db, n_pages // n),
            in_specs=[per_sample,
                      pl.BlockSpec((1, n, PAGE_SIZE), lambda b, j, pt: (b, j, 0)),
                      per_sample, per_sample,
                      pl.BlockSpec((1, N_HEADS, PAGE_SIZE), lambda b, j, pt: (b, 0, 0))]
            + _page_specs(head_tile, n) + _page_specs(head_tile, n),
            out_specs=pl.BlockSpec((1, N_HEADS, HEAD_DIM, 1), lambda b, j, pt: (b, 0, 0, 0)),
            scratch_shapes=[pltpu.VMEM((N_HEADS, HEAD_DIM, PAGE_SIZE), F32),
                            pltpu.VMEM((N_HEADS, PAGE_SIZE), F32), pltpu.VMEM((N_HEADS, PAGE_SIZE), F32)],
        ),
        out_shape=jax.ShapeDtypeStruct((db, N_HEADS, HEAD_DIM, 1), F32),
        compiler_params=_cparams(("parallel", "arbitrary")), name="sample_attend",
    )(page_table, qb, sel, k_new, v_new, sel_new, *([ck_t] * n), *([cv_t] * n))


def _rope_tables(pos):
    inv_freq = jnp.power(jnp.float32(ROPE_THETA), -jnp.arange(0, ROPE_ROT, 2, dtype=F32) / ROPE_ROT)
    ang = pos.astype(F32)[:, None] * inv_freq[None, :]
    return jnp.cos(ang).T, jnp.sin(ang).T


def _prepare_weights(p):
    w_in = p["w_in"]
    sizes = [ATTN_WIDTH, ATTN_WIDTH, ATTN_WIDTH, IDX_HEADS * IDX_DIM, IDX_DIM, IDX_HEADS,
             GMLP_WIDTH, GMLP_WIDTH, D_MODEL, D_MODEL]
    offs = np.concatenate([[0], np.cumsum(sizes)])
    n_attn = int(offs[6])
    wt_a_f32 = jnp.pad(w_in[:, :n_attn].T, ((0, _ROWS_A - n_attn), (0, 0)))
    wt_a = wt_a_f32.astype(BF16)
    wt_a_lo = (wt_a_f32 - wt_a.astype(F32)).astype(BF16)
    w_uv = w_in[:, int(offs[6]):int(offs[8])].astype(BF16)
    w_gates = w_in[:, int(offs[8]):].astype(BF16)
    w_re = jnp.transpose(p["w_router_expert"], (1, 0, 2)).reshape(D_MODEL, N_EXPERTS)
    w_router = jnp.pad(jnp.concatenate([w_re, p["w_router_group"]], axis=1),
                       ((0, 0), (0, LANES - N_EXPERTS - N_GROUPS)))
    b_router = jnp.pad(jnp.concatenate([p["b_router_expert"].reshape(-1), p["b_router_group"]]),
                       (0, LANES - N_EXPERTS - N_GROUPS)).reshape(1, LANES)
    return {
        "wt_a": wt_a, "wt_a_lo": wt_a_lo, "w_uv": w_uv, "w_gates": w_gates,
        "gate_bias": p["gate_bias"],
        "ln_v_g": p["ln_v_g"].reshape(1, -1), "ln_v_b": p["ln_v_b"].reshape(1, -1),
        "w_spatial": p["w_spatial"], "b_spatial_t": p["b_spatial"].T,
        "w_branch_a": p["w_branch_a"].astype(BF16), "w_branch_b": p["w_branch_b"].astype(BF16),
        "w_o": p["w_o"].astype(BF16),
        "ln1_g": p["ln1_g"].reshape(1, -1), "ln1_b": p["ln1_b"].reshape(1, -1),
        "ln2_g": p["ln2_g"].reshape(1, -1), "ln2_b": p["ln2_b"].reshape(1, -1),
        "w_router": w_router, "b_router": b_router,
        "w_expert_gate": p["w_expert_gate"].reshape(N_EXPERTS, D_MODEL, EXPERT_DIM).astype(BF16),
        "w_expert_up": p["w_expert_up"].reshape(N_EXPERTS, D_MODEL, EXPERT_DIM).astype(BF16),
        "w_expert_down": p["w_expert_down"].reshape(N_EXPERTS, EXPERT_DIM, D_MODEL).astype(BF16),
    }


def _pick_tile(n, pref):
    t = min(n, pref)
    assert n % t == 0, (n, pref)
    return t


def _channel_mix(x, attn, gm, p, *, precise_router):
    b, s, d = x.shape
    t = b * s
    hr = _merge(x, attn, gm, p, tm=_pick_tile(s, 512), precise_router=precise_router).reshape(t, d + LANES)
    if t % _MOE_TILE == 0 and t >= N_GROUPS * _MOE_TILE:
        y = _moe_sorted(hr, p)
    else:
        y = _moe(hr, p, tm=_pick_tile(t, 1024))
    return y.reshape(b, s, d)


def _prompt_step(x, p):
    b, s, _ = x.shape
    cos_t, sin_t = _rope_tables(jnp.arange(s, dtype=jnp.int32))
    tq = 256
    qz, kn, knb, vn, vtb, qit, kin, kinb, wit = _proj_attn(
        x, p["wt_a"], cos_t, sin_t, tm=512, tk=tq, q_scale=HEAD_DIM ** -0.5 * LOG2_E)
    attn = _prompt_attn(qz, knb, vtb, qit, kinb, wit, tq=tq)
    gm, _ = _gmlp(x, p["w_uv"], p["ln_v_g"], p["ln_v_b"], p["w_spatial"], p["b_spatial_t"],
                  tm=512, single_row_chunks=False)
    y = _channel_mix(x, attn, gm, p, precise_router=False)
    return y, _rows_from_t(kn), _rows_from_t(vn), jnp.transpose(kin, (0, 2, 1))


def _rows_from_t(a_t):
    b, _, s = a_t.shape
    return jnp.transpose(a_t.reshape(b, N_HEADS, HEAD_DIM, s), (0, 3, 1, 2))


def _sample_step(x, cache_k, cache_v, cache_kidx, page_table, p):
    db, t_new, d = x.shape
    assert t_new == 1
    n_pages = page_table.shape[1]
    past = n_pages * PAGE_SIZE
    rows = LANES
    assert db <= rows
    xs = jnp.pad(x.reshape(1, db, d), ((0, 0), (0, rows - db), (0, 0)))
    cos_t, sin_t = _rope_tables(jnp.full((rows,), past, jnp.int32))
    qz, kn, knb, vn, vtb, qit, kin, kinb, wit = _proj_attn(
        xs, p["wt_a"], cos_t, sin_t, tm=rows, tk=rows, q_scale=HEAD_DIM ** -0.5, wt_a_lo=p["wt_a_lo"])
    gm, chunk_v = _gmlp(xs, p["w_uv"], p["ln_v_g"], p["ln_v_b"], p["w_spatial"], p["b_spatial_t"],
                        tm=rows, single_row_chunks=True)

    qi_s = jnp.pad(qit[0].T.reshape(rows, IDX_HEADS, IDX_DIM), ((0, 0), (0, 8 - IDX_HEADS), (0, 0)))[:db]
    wi_s = jnp.broadcast_to(wit[0].T[:db, :, None], (db, 8, LANES))
    kidx_t = jnp.transpose(cache_kidx, (0, 2, 1))
    sc_past = _sample_scores(page_table, qi_s, wi_s, kidx_t,
                             n=_pick_tile(n_pages, _SCORE_PAGES_PER_STEP))
    own_page = jnp.pad(kin[0, :, :db].T[:, :, None], ((0, 0), (0, 0), (0, PAGE_SIZE - 1)))
    own_table = jnp.arange(db, dtype=jnp.int32).reshape(db, 1)
    sc_new = _sample_scores(own_table, qi_s, wi_s, own_page, n=1)[:, 0, 0]
    tk = 256
    nk = past + tk
    sc_all = jnp.concatenate([sc_past.reshape(db, past), sc_new[:, None],
                              jnp.full((db, tk - 1), -jnp.inf, F32)], axis=1)
    sc_t = jnp.pad(sc_all.T, ((0, 0), (0, rows - db)))
    n_valid = jnp.where(jnp.arange(rows) < db, float(past + 1), 0.0).astype(F32).reshape(1, rows)
    sel_t = _sample_select(sc_t, n_valid, tk=tk)
    sel = sel_t.T[:db]
    sel_past = sel[:, :past].reshape(db, n_pages, PAGE_SIZE)
    sel_new = jnp.broadcast_to(sel[:, past][:, None, None], (db, N_HEADS, PAGE_SIZE))

    def lane_replicated(a):
        return jnp.broadcast_to(a.astype(F32)[..., None], a.shape + (PAGE_SIZE,))

    q_s = jnp.stack([qz[0, hh, (hh % 2) * HEAD_DIM:(hh % 2 + 1) * HEAD_DIM, :db].T for hh in range(N_HEADS)],
                    axis=1)
    k_new = kn[0, :, :db].T.reshape(db, N_HEADS, HEAD_DIM)
    v_new = vn[0, :, :db].T.reshape(db, N_HEADS, HEAD_DIM)
    attn = _sample_attend(page_table, lane_replicated(q_s), sel_past, lane_replicated(k_new),
                          lane_replicated(v_new), sel_new,
                          jnp.transpose(cache_k, (0, 2, 3, 1)), jnp.transpose(cache_v, (0, 2, 3, 1)))
    attn_p = jnp.pad(attn.reshape(1, db, ATTN_WIDTH), ((0, 0), (0, rows - db), (0, 0))).astype(BF16)
    y = _channel_mix(xs, attn_p, gm, p, precise_router=True)
    return (y[0, :db].reshape(db, 1, d),
            k_new.reshape(db, 1, N_HEADS, HEAD_DIM), v_new.reshape(db, 1, N_HEADS, HEAD_DIM),
            kin[0, :, :db].T.reshape(db, 1, IDX_DIM), chunk_v[0, :db].reshape(db, 1, GMLP_WIDTH))


def kernel(x_prompt, x_sample, cache_k, cache_v, cache_kidx, page_table, w_in, gate_bias,
           ln_v_g, ln_v_b, w_spatial, b_spatial, w_branch_a, w_branch_b, w_o, ln1_g, ln1_b,
           w_router_group, b_router_group, w_router_expert, b_router_expert,
           w_expert_gate, w_expert_up, w_expert_down, ln2_g, ln2_b):
    p = _prepare_weights(dict(
        w_in=w_in, gate_bias=gate_bias, ln_v_g=ln_v_g, ln_v_b=ln_v_b, w_spatial=w_spatial,
        b_spatial=b_spatial, w_branch_a=w_branch_a, w_branch_b=w_branch_b, w_o=w_o, ln1_g=ln1_g,
        ln1_b=ln1_b, w_router_group=w_router_group, b_router_group=b_router_group,
        w_router_expert=w_router_expert, b_router_expert=b_router_expert,
        w_expert_gate=w_expert_gate, w_expert_up=w_expert_up, w_expert_down=w_expert_down,
        ln2_g=ln2_g, ln2_b=ln2_b))
    y_p, k_p, v_p, ki_p = _prompt_step(x_prompt, p)
    y_s, k_s, v_s, ki_s, cv_s = _sample_step(x_sample, cache_k, cache_v, cache_kidx, page_table, p)
    return (y_p, y_s, k_p, v_p, ki_p, k_s, v_s, ki_s, cv_s)
```
